```python
import math
import jax, jax.numpy as jnp
from jax import lax
import numpy as np

D_MODEL = 1024
BATCH = 4
SEQ = 4096
DEPTH = 2

GRID_W = 64
CTX_LEN = 256
N_BRANCH = 3
D_CONV = D_MODEL
CONV_WIDTH = 31
D_SSM = D_MODEL
SSM_GROUP = 16
SSM_GROUPS = D_SSM // SSM_GROUP
SSM_STATE = 64
DT_MIN = 1e-3
DT_MAX = 1e-1
ATTN_HEADS = 8
ATTN_DH = 64
ATTN_DV = 2 * ATTN_DH
QK_W = ATTN_HEADS * 2 * ATTN_DH
D_ATTN = ATTN_HEADS * ATTN_DV
Q_BLOCK = 128
ROPE_BASE = 10000.0
N_EXPERTS = 16
D_EXPERT = 2816
CAPACITY_FACTOR = 2
DEEPNORM_ALPHA = (2.0 * DEPTH) ** 0.25
DEEPNORM_BETA = (8.0 * DEPTH) ** -0.25
LN_EPS = 1e-6
RMS_EPS = 1e-5
CONV_OFF = 0
GATE_OFF = CONV_OFF + 2 * D_CONV
Q_OFF = GATE_OFF + N_BRANCH * D_MODEL
SSM_OFF = Q_OFF + QK_W
K_OFF = SSM_OFF + D_SSM
V_OFF = K_OFF + QK_W
D_IN = V_OFF + D_ATTN

kernel_name = 'hybrid_conv_s5_diffattn_ec_moe_block'


def layer_norm(x, gain=None, bias=None):
    xf = x.astype(jnp.float32)
    mu = jnp.mean(xf, -1, keepdims=True)
    var = jnp.mean(jnp.square(xf - mu), -1, keepdims=True)
    y = (xf - mu) * lax.rsqrt(var + LN_EPS)
    if gain is not None:
        y = y * gain.astype(jnp.float32) + bias.astype(jnp.float32)
    return y.astype(x.dtype)


def rms_norm(x, gain):
    xf = x.astype(jnp.float32)
    y = xf * lax.rsqrt(jnp.mean(jnp.square(xf), -1, keepdims=True) + RMS_EPS) * gain.astype(jnp.float32)
    return y.astype(x.dtype)


def axial_rope_tables(length):
    rows = length // GRID_W
    row = jnp.repeat(jnp.arange(rows), GRID_W)
    col = jnp.tile(jnp.arange(GRID_W), rows)
    n_freq = ATTN_DH // 4
    inv_freq = ROPE_BASE ** (-jnp.arange(n_freq, dtype=jnp.float32) / n_freq)
    ang = jnp.stack([row, col], -1).astype(jnp.float32)[:, :, None, None] * inv_freq
    ang = jnp.broadcast_to(ang, (length, 2, 2, n_freq)).reshape(length, ATTN_DH)
    return jnp.cos(ang), jnp.sin(ang)


def apply_rope(x, cos, sin):
    xa = x.reshape(x.shape[:-1] + (2, 2, ATTN_DH // 4))
    rot = jnp.concatenate([-xa[..., 1:, :], xa[..., :1, :]], axis=-2).reshape(x.shape)
    return (x * cos[:, None, None, :] + rot * sin[:, None, None, :]).astype(x.dtype)


def diff_attention(q, k, v, lam):
    s = jnp.einsum('bqhsd,bkhsd->bhsqk', q, k).astype(jnp.float32) * (ATTN_DH ** -0.5)
    p = jax.nn.softmax(s, axis=-1)
    w = (p[:, :, 0] - lam * p[:, :, 1]).astype(v.dtype)
    return jnp.einsum('bhqk,bkhe->bqhe', w, v)


def blocked_diff_attention(q, k, v, lam):
    b, l = q.shape[:2]
    nb = l // Q_BLOCK
    qb = q.reshape((b, nb, Q_BLOCK) + q.shape[2:]).swapaxes(0, 1)
    o = lax.map(lambda qi: diff_attention(qi, k, v, lam), qb)
    return o.swapaxes(0, 1).reshape((b, l) + o.shape[3:])


def attn_output(o, p, lam_init):
    o = rms_norm(o, p['attn_subln_g']) * (1.0 - lam_init)
    return o.reshape(o.shape[:2] + (D_ATTN,)) @ p['w_attn_out']


def conv_branch(zconv, p):
    g = zconv[..., :D_CONV] * jax.nn.sigmoid(zconv[..., D_CONV:])
    half = CONV_WIDTH // 2
    y = lax.conv_general_dilated(g, p['conv_w'][:, None, :], (1,), [(half, half)],
                                 dimension_numbers=('NWC', 'WIO', 'NWC'),
                                 feature_group_count=D_CONV) + p['conv_b']
    y = jax.nn.silu(layer_norm(y, p['conv_ln_g'], p['conv_ln_b']))
    return y @ p['w_conv_out']


def s5_discretise(a_re, a_im, log_dt, b_re, b_im):
    a = lax.complex(a_re.astype(jnp.float32), a_im.astype(jnp.float32))
    dt_a = jnp.exp(log_dt.astype(jnp.float32))[:, None] * a
    a_bar = jnp.exp(dt_a)
    b = lax.complex(b_re.astype(jnp.float32), b_im.astype(jnp.float32))
    b_bar = ((a_bar - 1.0) / a)[..., None] * b
    return dt_a, a_bar, b_bar


def _ssm_combine(left, right):
    a_l, b_l = left
    a_r, b_r = right
    return a_l * a_r, a_r * b_l + b_r


def s5_states(u, dt_a, a_bar, b_bar, reverse, h0=None):
    length = u.shape[1]
    bu = jnp.einsum('blgc,gpc->blgp', u.astype(jnp.complex64), b_bar)
    a = jnp.broadcast_to(a_bar, (1, length) + a_bar.shape)
    _, s = lax.associative_scan(_ssm_combine, (a, bu), reverse=reverse, axis=1)
    if h0 is not None:
        steps = jnp.arange(length, 0, -1) if reverse else jnp.arange(1, length + 1)
        s = s + jnp.exp(steps.astype(jnp.float32)[:, None, None] * dt_a)[None] * h0[:, None]
    return s


def s5_readout(s, c_mat):
    return jnp.real(jnp.einsum('blgp,gcp->blgc', s, c_mat))


def ssm_scans(u, u_c, p, ctx_out):
    b, l, _ = u.shape
    ug = u.reshape(b, l, SSM_GROUPS, SSM_GROUP)
    ucg = u_c.reshape(b, u_c.shape[1], SSM_GROUPS, SSM_GROUP)
    y_lat, y_ctx = [], []
    for d, rev in enumerate((False, True)):
        dt_a, a_bar, b_bar = s5_discretise(p['ssm_a_re'][d], p['ssm_a_im'][d], p['ssm_log_dt'][d],
                                           p['ssm_b_re'][d], p['ssm_b_im'][d])
        c_mat = lax.complex(p['ssm_c_re'][d].astype(jnp.float32), p['ssm_c_im'][d].astype(jnp.float32))
        s_ctx = s5_states(ucg, dt_a, a_bar, b_bar, rev)
        h0 = s_ctx[:, 0] if rev else s_ctx[:, -1]
        y_lat.append(s5_readout(s5_states(ug, dt_a, a_bar, b_bar, rev, h0), c_mat))
        if ctx_out:
            y_ctx.append(s5_readout(s_ctx, c_mat))
    return y_lat[0] + y_lat[1], (y_ctx[0] + y_ctx[1] if ctx_out else None)


def s5_output(y, u, p):
    y = y.reshape(u.shape).astype(u.dtype) + p['ssm_d'] * u
    y = jax.nn.gelu(y)
    y = y * jax.nn.sigmoid(y @ p['w_ssm_glu'])
    return y @ p['w_ssm_out']


def split_side(zs):
    b, n = zs.shape[:2]
    u = zs[..., :D_SSM]
    k = zs[..., K_OFF - SSM_OFF:V_OFF - SSM_OFF].reshape(b, n, ATTN_HEADS, 2, ATTN_DH)
    v = zs[..., V_OFF - SSM_OFF:].reshape(b, n, ATTN_HEADS, ATTN_DV)
    return u, k, v


def merge(zg, y_conv, y_ssm, y_attn, w_o):
    g = jax.nn.sigmoid(zg).reshape(zg.shape[:-1] + (N_BRANCH, D_MODEL))
    m = g[..., 0, :] * y_conv + g[..., 1, :] * y_ssm + g[..., 2, :] * y_attn
    return m @ w_o


def token_mixer(h, hc, p, layer_idx, cos, sin, ctx_out):
    b, l, _ = h.shape
    lc = hc.shape[1]
    z = h @ p['w_in']
    zc = hc @ (p['w_in'] if ctx_out else p['w_in'][:, SSM_OFF:])
    u, k, v = split_side(z[..., SSM_OFF:])
    u_c, k_c, v_c = split_side(zc[..., SSM_OFF:] if ctx_out else zc)
    y_conv = conv_branch(z[..., CONV_OFF:GATE_OFF], p)
    y_s_lat, y_s_ctx = ssm_scans(u, u_c, p, ctx_out)
    y_ssm = s5_output(y_s_lat, u, p)
    lam_init = 0.8 - 0.6 * math.exp(-0.3 * layer_idx)
    lq1, lk1, lq2, lk2 = [p['attn_lambda'][i].astype(jnp.float32) for i in range(4)]
    lam = jnp.exp(jnp.sum(lq1 * lk1)) - jnp.exp(jnp.sum(lq2 * lk2)) + lam_init
    q = apply_rope(z[..., Q_OFF:SSM_OFF].reshape(b, l, ATTN_HEADS, 2, ATTN_DH), cos, sin)
    k = apply_rope(k, cos, sin)
    o = blocked_diff_attention(q, jnp.concatenate([k_c, k], 1), jnp.concatenate([v_c, v], 1), lam)
    y_attn = attn_output(o, p, lam_init)
    y = merge(z[..., GATE_OFF:Q_OFF], y_conv, y_ssm, y_attn, p['w_o'])
    if not ctx_out:
        return y, None
    q_c = zc[..., Q_OFF:SSM_OFF].reshape(b, lc, ATTN_HEADS, 2, ATTN_DH)
    o_c = diff_attention(q_c, k_c, v_c, lam)
    y_c = merge(zc[..., GATE_OFF:Q_OFF], conv_branch(zc[..., CONV_OFF:GATE_OFF], p),
                s5_output(y_s_ctx, u_c, p), attn_output(o_c, p, lam_init), p['w_o'])
    return y, y_c


def expert_choice_moe(h, w_router, w_gate_up, w_down):
    b, n, d = h.shape
    cap = CAPACITY_FACTOR * n // N_EXPERTS
    aff = jax.nn.softmax((h @ w_router).astype(jnp.float32), axis=-1)
    gate, idx = lax.top_k(aff.swapaxes(1, 2), cap)
    xs = jax.vmap(lambda hb, ib: hb[ib])(h, idx)
    gu = jnp.einsum('becd,edf->becf', xs, w_gate_up)
    a, up = jnp.split(gu, 2, axis=-1)
    ye = jnp.einsum('becf,efd->becd', jax.nn.silu(a) * up, w_down) * gate[..., None].astype(h.dtype)
    return jax.vmap(lambda yb, ib: jnp.zeros((n, d), h.dtype).at[ib.reshape(-1)].add(yb.reshape(-1, d)))(ye, idx)


def trunk_layer(x, xc, c, c_ctx, p, layer_idx, cos, sin, ctx_out):
    mod = (jax.nn.silu(c) @ p['w_ada'] + p['b_ada'])[:, None, :]
    mod_c = (jax.nn.silu(c_ctx) @ p['w_ada'] + p['b_ada'])[None, None, :]
    sh1, sc1, g1, sh2, sc2, g2 = jnp.split(mod, 6, axis=-1)
    csh1, csc1, cg1, csh2, csc2, cg2 = jnp.split(mod_c, 6, axis=-1)
    h = layer_norm(x) * (1.0 + sc1) + sh1
    hc = layer_norm(xc) * (1.0 + csc1) + csh1
    y, y_c = token_mixer(h, hc, p, layer_idx, cos, sin, ctx_out)
    x = layer_norm(DEEPNORM_ALPHA * x + g1 * y, p['ln1_g'], p['ln1_b'])
    h2 = layer_norm(x) * (1.0 + sc2) + sh2
    x = layer_norm(DEEPNORM_ALPHA * x + g2 * expert_choice_moe(h2, p['w_router'], p['w_gate_up'], p['w_down']),
                   p['ln2_g'], p['ln2_b'])
    if ctx_out:
        xc = layer_norm(DEEPNORM_ALPHA * xc + cg1 * y_c, p['ln1_g'], p['ln1_b'])
        hc2 = layer_norm(xc) * (1.0 + csc2) + csh2
        xc = layer_norm(DEEPNORM_ALPHA * xc + cg2 * expert_choice_moe(hc2, p['w_router'], p['w_gate_up'], p['w_down']),
                        p['ln2_g'], p['ln2_b'])
    return x, xc


def setup_inputs(seed: int = 0) -> dict:
    key = jax.random.key(seed)
    ks = iter(jax.random.split(key, 40))
    f32 = jnp.float32

    def nrm(shape, std):
        return std * jax.random.normal(next(ks), shape, f32)

    return {
        'x': nrm((BATCH, SEQ, D_MODEL), 1.0),
        'c': nrm((BATCH, D_MODEL), 1.0),
        'ctx': nrm((BATCH, CTX_LEN, D_MODEL), 1.0),
        'c_ctx': nrm((D_MODEL,), 1.0),
        'w_ada': nrm((DEPTH, D_MODEL, 6 * D_MODEL), 0.3 * D_MODEL ** -0.5),
        'b_ada': nrm((DEPTH, 6 * D_MODEL), 0.02),
        'w_in': nrm((DEPTH, D_MODEL, D_IN), D_MODEL ** -0.5),
        'conv_w': nrm((DEPTH, CONV_WIDTH, D_CONV), CONV_WIDTH ** -0.5),
        'conv_b': nrm((DEPTH, D_CONV), 0.02),
        'conv_ln_g': 1.0 + nrm((DEPTH, D_CONV), 0.02),
        'conv_ln_b': nrm((DEPTH, D_CONV), 0.02),
        'w_conv_out': nrm((DEPTH, D_CONV, D_MODEL), D_CONV ** -0.5),
        'ssm_a_re': -0.5 + nrm((DEPTH, 2, SSM_GROUPS, SSM_STATE), 0.01),
        'ssm_a_im': jnp.pi * jnp.arange(SSM_STATE, dtype=f32) + nrm((DEPTH, 2, SSM_GROUPS, SSM_STATE), 0.01),
        'ssm_log_dt': jax.random.uniform(next(ks), (DEPTH, 2, SSM_GROUPS), f32,
                                         minval=math.log(DT_MIN), maxval=math.log(DT_MAX)),
        'ssm_b_re': nrm((DEPTH, 2, SSM_GROUPS, SSM_STATE, SSM_GROUP), (2.0 * SSM_GROUP) ** -0.5),
        'ssm_b_im': nrm((DEPTH, 2, SSM_GROUPS, SSM_STATE, SSM_GROUP), (2.0 * SSM_GROUP) ** -0.5),
        'ssm_c_re': nrm((DEPTH, 2, SSM_GROUPS, SSM_GROUP, SSM_STATE), 0.5),
        'ssm_c_im': nrm((DEPTH, 2, SSM_GROUPS, SSM_GROUP, SSM_STATE), 0.5),
        'ssm_d': nrm((DEPTH, D_SSM), 0.5),
        'w_ssm_glu': nrm((DEPTH, D_SSM, D_SSM), D_SSM ** -0.5),
        'w_ssm_out': nrm((DEPTH, D_SSM, D_MODEL), D_SSM ** -0.5),
        'attn_lambda': nrm((DEPTH, 4, ATTN_DH), 0.1),
        'attn_subln_g': 1.0 + nrm((DEPTH, ATTN_DV), 0.02),
        'w_attn_out': nrm((DEPTH, D_ATTN, D_MODEL), D_ATTN ** -0.5),
        'w_o': nrm((DEPTH, D_MODEL, D_MODEL), DEEPNORM_BETA * D_MODEL ** -0.5),
        'ln1_g': 1.0 + nrm((DEPTH, D_MODEL), 0.02),
        'ln1_b': nrm((DEPTH, D_MODEL), 0.02),
        'w_router': nrm((DEPTH, D_MODEL, N_EXPERTS), D_MODEL ** -0.5),
        'w_gate_up': nrm((DEPTH, N_EXPERTS, D_MODEL, 2 * D_EXPERT), D_MODEL ** -0.5),
        'w_down': nrm((DEPTH, N_EXPERTS, D_EXPERT, D_MODEL), DEEPNORM_BETA * D_EXPERT ** -0.5),
        'ln2_g': 1.0 + nrm((DEPTH, D_MODEL), 0.02),
        'ln2_b': nrm((DEPTH, D_MODEL), 0.02),
    }


def reference(x, c, ctx, c_ctx, w_ada, b_ada, w_in, conv_w, conv_b, conv_ln_g, conv_ln_b, w_conv_out,
              ssm_a_re, ssm_a_im, ssm_log_dt, ssm_b_re, ssm_b_im, ssm_c_re, ssm_c_im, ssm_d, w_ssm_glu,
              w_ssm_out, attn_lambda, attn_subln_g, w_attn_out, w_o, ln1_g, ln1_b, w_router, w_gate_up,
              w_down, ln2_g, ln2_b):
    cos, sin = axial_rope_tables(x.shape[1])
    xc = ctx
    for l in range(DEPTH):
        p = {
            'w_ada': w_ada[l], 'b_ada': b_ada[l], 'w_in': w_in[l],
            'conv_w': conv_w[l], 'conv_b': conv_b[l], 'conv_ln_g': conv_ln_g[l], 'conv_ln_b': conv_ln_b[l],
            'w_conv_out': w_conv_out[l],
            'ssm_a_re': ssm_a_re[l], 'ssm_a_im': ssm_a_im[l], 'ssm_log_dt': ssm_log_dt[l],
            'ssm_b_re': ssm_b_re[l], 'ssm_b_im': ssm_b_im[l], 'ssm_c_re': ssm_c_re[l], 'ssm_c_im': ssm_c_im[l],
            'ssm_d': ssm_d[l], 'w_ssm_glu': w_ssm_glu[l], 'w_ssm_out': w_ssm_out[l],
            'attn_lambda': attn_lambda[l], 'attn_subln_g': attn_subln_g[l], 'w_attn_out': w_attn_out[l],
            'w_o': w_o[l], 'ln1_g': ln1_g[l], 'ln1_b': ln1_b[l],
            'w_router': w_router[l], 'w_gate_up': w_gate_up[l], 'w_down': w_down[l],
            'ln2_g': ln2_g[l], 'ln2_b': ln2_b[l],
        }
        x, xc = trunk_layer(x, xc, c, c_ctx, p, l, cos, sin, ctx_out=(l < DEPTH - 1))
    return x
```

```python
import functools
import math

import jax
import jax.numpy as jnp
import numpy as np
from jax import lax
from jax.experimental import pallas as pl
from jax.experimental.pallas import tpu as pltpu

F32 = jnp.float32
BF16 = jnp.bfloat16

D_MODEL = 1024
GRID_W = 64
N_BRANCH = 3
CONV_WIDTH = 31
CONV_HALO = 16
SSM_GROUP = 16
SSM_GROUPS = D_MODEL // SSM_GROUP
SSM_STATE = 64
SSM_SETS = 4
SSM_SET_CH = D_MODEL // SSM_SETS
SSM_SET_ST = SSM_GROUPS * SSM_STATE // SSM_SETS
ATTN_HEADS = 8
ATTN_DH = 64
ATTN_DV = 128
ROPE_BASE = 10000.0
N_EXPERTS = 16
D_EXPERT = 2816
CAPACITY_FACTOR = 2
LN_EPS = 1e-6
RMS_EPS = 1e-5
LANES = 128

COL_CONV_V, COL_CONV_G, COL_GATE0, COL_Q, COL_U, COL_K, COL_V = 0, 1, 2, 5, 6, 7, 8
N_COLS = 9

VMEM_LIMIT = 56 * 1024 * 1024


def _cparams(sem, vmem=VMEM_LIMIT):
    return pltpu.CompilerParams(dimension_semantics=sem, vmem_limit_bytes=vmem)


def _ln(x):
    mu = jnp.mean(x, axis=-1, keepdims=True)
    xc = x - mu
    var = jnp.mean(xc * xc, axis=-1, keepdims=True)
    return xc * lax.rsqrt(var + LN_EPS)


def _sigmoid(x):
    return 1.0 / (1.0 + jnp.exp(-x))


def _dot(a, b):
    return jnp.dot(a, b, preferred_element_type=F32)


def _dot_nt(a, b):
    return lax.dot_general(a, b, (((1,), (1,)), ((), ())), preferred_element_type=F32)


def _split_bf16(x):
    hi = x.astype(BF16)
    lo = (x - hi.astype(F32)).astype(BF16)
    return hi, lo


def _ada_kernel(c_ref, w_ref, b_ref, o_ref):
    c = c_ref[...]
    s = c * _sigmoid(c)
    hi, lo = _split_bf16(s)
    whi, wlo = _split_bf16(w_ref[...])
    o_ref[...] = _dot(hi, whi) + _dot(hi, wlo) + _dot(lo, whi) + b_ref[...]


def ada_modulation(cc, w_ada, b_ada):
    n = w_ada.shape[1]
    tn = 1024
    return pl.pallas_call(
        _ada_kernel,
        out_shape=jax.ShapeDtypeStruct((8, n), F32),
        grid=(n // tn,),
        in_specs=[pl.BlockSpec((8, D_MODEL), lambda j: (0, 0)),
                  pl.BlockSpec((D_MODEL, tn), lambda j: (0, j)),
                  pl.BlockSpec((1, tn), lambda j: (0, j))],
        out_specs=pl.BlockSpec((8, tn), lambda j: (0, j)),
        compiler_params=_cparams(("arbitrary",)),
        name="ada_modulation",
    )(cc, w_ada, b_ada.reshape(1, n))


def _inproj_kernel(x_ref, sc_ref, sh_ref, w_ref, cos_ref, sa_ref, sb_ref, o_ref, h_ref, *, rope):
    j = pl.program_id(1)

    @pl.when(j == 0)
    def _():
        h = _ln(x_ref[...]) * (1.0 + sc_ref[...]) + sh_ref[...]
        h_ref[...] = h.astype(BF16)

    acc = _dot(h_ref[...], w_ref[...])
    tn = acc.shape[1]

    def roped(scale):
        cos, sa, sb = cos_ref[...], sa_ref[...], sb_ref[...]
        for hh in range(tn // LANES):
            seg = acc[:, hh * LANES:(hh + 1) * LANES]
            r = seg * cos + pltpu.roll(seg, LANES - 16, 1) * sa + pltpu.roll(seg, 16, 1) * sb
            o_ref[:, hh * LANES:(hh + 1) * LANES] = (r * scale).astype(o_ref.dtype)

    is_q = j == COL_Q
    is_k = j == COL_K
    if rope:
        @pl.when(is_q)
        def _():
            roped(ATTN_DH ** -0.5)

        @pl.when(is_k)
        def _():
            roped(1.0)

        @pl.when(jnp.logical_not(is_q | is_k))
        def _():
            o_ref[...] = acc.astype(o_ref.dtype)
    else:
        @pl.when(is_q)
        def _():
            o_ref[...] = (acc * (ATTN_DH ** -0.5)).astype(o_ref.dtype)

        @pl.when(jnp.logical_not(is_q))
        def _():
            o_ref[...] = acc.astype(o_ref.dtype)


def input_projection(x2d, sc, sh, w_bf16, rope_tabs, seq_len, rope):
    m = x2d.shape[0]
    tm = min(1024, seq_len)
    tps = seq_len // tm
    nb = sc.shape[0]
    cos, sa, sb = rope_tabs
    mod_map = (lambda i, j: (i // tps, 0, 0)) if nb > 1 else (lambda i, j: (0, 0, 0))
    tab_map = lambda i, j: (i % tps, 0)
    return pl.pallas_call(
        functools.partial(_inproj_kernel, rope=rope),
        out_shape=jax.ShapeDtypeStruct((m, N_COLS * D_MODEL), BF16),
        grid=(m // tm, N_COLS),
        in_specs=[pl.BlockSpec((tm, D_MODEL), lambda i, j: (i, 0)),
                  pl.BlockSpec((None, 1, D_MODEL), mod_map),
                  pl.BlockSpec((None, 1, D_MODEL), mod_map),
                  pl.BlockSpec((D_MODEL, D_MODEL), lambda i, j: (0, j)),
                  pl.BlockSpec((tm, LANES), tab_map),
                  pl.BlockSpec((tm, LANES), tab_map),
                  pl.BlockSpec((tm, LANES), tab_map)],
        out_specs=pl.BlockSpec((tm, D_MODEL), lambda i, j: (i, j)),
        scratch_shapes=[pltpu.VMEM((tm, D_MODEL), BF16)],
        compiler_params=_cparams(("parallel", "arbitrary")),
        name="input_projection",
    )(x2d, sc, sh, w_bf16, cos, sa, sb)


def rope_tables(length):
    rows = length // GRID_W
    row = jnp.repeat(jnp.arange(rows), GRID_W)
    col = jnp.tile(jnp.arange(GRID_W), rows)
    n_freq = ATTN_DH // 4
    inv_freq = ROPE_BASE ** (-jnp.arange(n_freq, dtype=F32) / n_freq)
    ang = jnp.stack([row, col], -1).astype(F32)[:, :, None, None] * inv_freq
    ang = jnp.broadcast_to(ang, (length, 2, 2, n_freq)).reshape(length, ATTN_DH)
    cos, sin = jnp.cos(ang), jnp.sin(ang)
    first_half = (jnp.arange(ATTN_DH) % 32) < 16
    sa = jnp.where(first_half, -sin, 0.0)
    sb = jnp.where(first_half, 0.0, sin)
    tile2 = lambda t: jnp.concatenate([t, t], axis=-1)
    return tile2(cos), tile2(sa), tile2(sb)


def _conv_kernel(cur_ref, prev_ref, next_ref, w_ref, b_ref, g_ref, beta_ref, o_ref, gbuf, ybuf, *, tps):
    ti = pl.program_id(0) % tps
    tm = cur_ref.shape[0]
    d = D_MODEL

    def glu(ref):
        return ref[:, :d].astype(F32) * _sigmoid(ref[:, d:].astype(F32))

    gbuf[CONV_HALO:CONV_HALO + tm, :] = glu(cur_ref)
    gbuf[0:CONV_HALO, :] = jnp.where(ti > 0, glu(prev_ref), 0.0)
    gbuf[CONV_HALO + tm:, :] = jnp.where(ti < tps - 1, glu(next_ref), 0.0)
    off = CONV_HALO - CONV_WIDTH // 2
    for c in range(d // LANES):
        cs = slice(c * LANES, (c + 1) * LANES)
        acc = jnp.zeros((tm, LANES), F32)
        for k in range(CONV_WIDTH):
            acc = acc + gbuf[off + k:off + k + tm, cs] * w_ref[k:k + 1, cs]
        ybuf[:, cs] = acc
    y = _ln(ybuf[...] + b_ref[...]) * g_ref[...] + beta_ref[...]
    o_ref[...] = (y * _sigmoid(y)).astype(o_ref.dtype)


def conv_branch(z, seq_len, conv_w, conv_b, ln_g, ln_b):
    m = z.shape[0]
    tm = min(256, seq_len)
    tps = seq_len // tm
    hb = tm // CONV_HALO
    nblk = m // CONV_HALO
    row = lambda a: a.reshape(1, D_MODEL)
    return pl.pallas_call(
        functools.partial(_conv_kernel, tps=tps),
        out_shape=jax.ShapeDtypeStruct((m, D_MODEL), BF16),
        grid=(m // tm,),
        in_specs=[pl.BlockSpec((tm, 2 * D_MODEL), lambda i: (i, 0)),
                  pl.BlockSpec((CONV_HALO, 2 * D_MODEL), lambda i: (jnp.maximum(i * hb - 1, 0), 0)),
                  pl.BlockSpec((CONV_HALO, 2 * D_MODEL), lambda i: (jnp.minimum((i + 1) * hb, nblk - 1), 0)),
                  pl.BlockSpec((CONV_WIDTH, D_MODEL), lambda i: (0, 0)),
                  pl.BlockSpec((1, D_MODEL), lambda i: (0, 0)),
                  pl.BlockSpec((1, D_MODEL), lambda i: (0, 0)),
                  pl.BlockSpec((1, D_MODEL), lambda i: (0, 0))],
        out_specs=pl.BlockSpec((tm, D_MODEL), lambda i: (i, 0)),
        scratch_shapes=[pltpu.VMEM((tm + 2 * CONV_HALO, D_MODEL), F32), pltpu.VMEM((tm, D_MODEL), F32)],
        compiler_params=_cparams(("parallel",)),
        name="conv_branch",
    )(z, z, z, conv_w, row(conv_b), row(ln_g), row(ln_b))


def ssm_prepare(a_re, a_im, log_dt, b_re, b_im, c_re, c_im, batch):
    a = lax.complex(a_re.astype(F32), a_im.astype(F32))
    dt_a = jnp.exp(log_dt.astype(F32))[..., None] * a
    a_bar = jnp.exp(dt_a)
    b = lax.complex(b_re.astype(F32), b_im.astype(F32))
    b_bar = ((a_bar - 1.0) / a)[..., None] * b
    gl = SSM_GROUPS // SSM_SETS

    def state_cols(t):
        re = jnp.real(t).reshape(2, SSM_SETS, SSM_SET_ST)
        im = jnp.imag(t).reshape(2, SSM_SETS, SSM_SET_ST)
        return jnp.concatenate([re, im], axis=-1).reshape(2, SSM_SETS * 2 * SSM_SET_ST)

    a_cols = state_cols(a_bar)
    a_rows = jnp.repeat(a_cols, batch, axis=0)

    eye = jnp.eye(gl, dtype=F32)
    bb = b_bar.reshape(2, SSM_SETS, gl, SSM_STATE, SSM_GROUP)
    def bmat(part):
        t = jnp.einsum('dsgpc,gh->sdgchp', part, eye)
        return t.reshape(SSM_SETS, 2 * gl * SSM_GROUP, gl * SSM_STATE)
    bw = jnp.concatenate([bmat(jnp.real(bb)), bmat(jnp.imag(bb))], axis=-1).astype(BF16)
    cc = lax.complex(c_re.astype(F32), c_im.astype(F32)).reshape(2, SSM_SETS, gl, SSM_GROUP, SSM_STATE)
    def cmat(part):
        t = jnp.einsum('dsgcp,gh->sgpdhc', part, eye)
        return t.reshape(SSM_SETS, gl * SSM_STATE, 2 * gl * SSM_GROUP)
    cw = jnp.concatenate([cmat(jnp.real(cc)), cmat(-jnp.imag(cc))], axis=1).astype(BF16)
    return a_rows, bw, cw


def ssm_permutations(batch, tc):
    n_seq = 2 * batch
    pf = np.zeros((n_seq * tc, batch * tc), np.float32)
    pr = np.zeros((n_seq * tc, batch * tc), np.float32)
    for t in range(tc):
        for b in range(batch):
            pf[t * n_seq + b, b * tc + t] = 1.0
            pr[t * n_seq + batch + b, b * tc + (tc - 1 - t)] = 1.0
    to = lambda a: jnp.asarray(a, BF16)
    return to(pf), to(pr), to(pf.T), to(pr.T)


def _ssm_kernel(uf_ref, ur_ref, h0_ref, a_ref, bw_ref, cw_ref, pf_ref, pr_ref, pft_ref, prt_ref,
                yf_ref, yr_ref, hend_ref, hbuf, *, write_y):
    c = pl.program_id(0)
    batch, tc, _ = uf_ref.shape
    n_seq = 2 * batch
    rows = n_seq * tc
    set_w = 2 * SSM_SET_ST

    @pl.when(c == 0)
    def _():
        hend_ref[...] = h0_ref[...]

    uf = uf_ref[...].reshape(batch * tc, D_MODEL)
    ur = ur_ref[...].reshape(batch * tc, D_MODEL)
    u8f = _dot(pf_ref[...], uf).astype(BF16)
    u8r = _dot(pr_ref[...], ur).astype(BF16)
    for s in range(SSM_SETS):
        cs = slice(s * SSM_SET_CH, (s + 1) * SSM_SET_CH)
        lhs = jnp.concatenate([u8f[:, cs], u8r[:, cs]], axis=1)
        bu = _dot(lhs, bw_ref[s])
        hbuf[:, :, s * set_w:(s + 1) * set_w] = bu.reshape(tc, n_seq, set_w)

    for s in range(SSM_SETS):
        re = slice(s * set_w, s * set_w + SSM_SET_ST)
        im = slice(s * set_w + SSM_SET_ST, (s + 1) * set_w)
        ar, ai = a_ref[:, re], a_ref[:, im]

        def step(t, carry):
            hr, hi = carry
            nr = ar * hr - ai * hi + hbuf[t, :, re]
            ni = ar * hi + ai * hr + hbuf[t, :, im]
            hbuf[t, :, re] = nr
            hbuf[t, :, im] = ni
            return nr, ni

        hr, hi = lax.fori_loop(0, tc, step, (hend_ref[:, re], hend_ref[:, im]), unroll=8)
        hend_ref[:, re] = hr
        hend_ref[:, im] = hi

    if write_y:
        y8f, y8r = [], []
        for s in range(SSM_SETS):
            st = hbuf[:, :, s * set_w:(s + 1) * set_w].reshape(rows, set_w).astype(BF16)
            y = _dot(st, cw_ref[s])
            y8f.append(y[:, :SSM_SET_CH])
            y8r.append(y[:, SSM_SET_CH:])
        for parts, pt_ref, y_ref in ((y8f, pft_ref, yf_ref), (y8r, prt_ref, yr_ref)):
            hi, lo = _split_bf16(jnp.concatenate(parts, axis=1))
            y = _dot(pt_ref[...], hi) + _dot(pt_ref[...], lo)
            y_ref[...] = y.reshape(batch, tc, D_MODEL)
    else:
        yf_ref[...] = jnp.zeros(yf_ref.shape, F32)
        yr_ref[...] = jnp.zeros(yr_ref.shape, F32)


def ssm_scan(z3, h0, a_rows, bw, cw, write_y):
    batch, seq_len, _ = z3.shape
    tc = min(64, seq_len)
    nc = seq_len // tc
    n_state = 2 * SSM_GROUPS * SSM_STATE
    perms = ssm_permutations(batch, tc)
    const = lambda shape: pl.BlockSpec(shape, lambda c: (0,) * len(shape))
    yshape = jax.ShapeDtypeStruct((batch, seq_len, D_MODEL), F32)
    return pl.pallas_call(
        functools.partial(_ssm_kernel, write_y=write_y),
        out_shape=(yshape, yshape, jax.ShapeDtypeStruct((2 * batch, n_state), F32)),
        grid=(nc,),
        in_specs=[pl.BlockSpec((batch, tc, D_MODEL), lambda c: (0, c, COL_U)),
                  pl.BlockSpec((batch, tc, D_MODEL), lambda c: (0, nc - 1 - c, COL_U)),
                  const((2 * batch, n_state)), const((2 * batch, n_state)),
                  const(bw.shape), const(cw.shape)] + [const(p.shape) for p in perms],
        out_specs=(pl.BlockSpec((batch, tc, D_MODEL), lambda c: (0, c, 0)),
                   pl.BlockSpec((batch, tc, D_MODEL), lambda c: (0, nc - 1 - c, 0)),
                   const((2 * batch, n_state))),
        scratch_shapes=[pltpu.VMEM((tc, 2 * batch, n_state), F32)],
        compiler_params=_cparams(("arbitrary",)),
        name="ssm_scan",
    )(z3, z3, h0, a_rows, bw, cw, *perms)


def _attn_kernel(*refs, n_src, lam_init):
    lam_ref, g_ref, q_ref = refs[:3]
    k_refs = refs[3:3 + n_src]
    v_refs = refs[3 + n_src:3 + 2 * n_src]
    o_ref, vext = refs[3 + 2 * n_src], refs[4 + 2 * n_src]
    qi = pl.program_id(2)

    @pl.when(qi == 0)
    def _():
        r0 = 0
        for v_ref in v_refs:
            n = v_ref.shape[0]
            vext[r0:r0 + n, :ATTN_DV] = v_ref[...]
            vext[r0:r0 + n, ATTN_DV:] = jnp.ones((n, ATTN_DV), BF16)
            r0 += n

    q = q_ref[...]
    lane = lax.broadcasted_iota(jnp.int32, q.shape, 1)
    outs = []
    for sub in range(2):
        qs = jnp.where((lane >= ATTN_DH) if sub else (lane < ATTN_DH), q, jnp.zeros_like(q))
        scores = [_dot_nt(qs, k_ref[...]) for k_ref in k_refs]
        m = functools.reduce(jnp.maximum, [jnp.max(s, axis=-1, keepdims=True) for s in scores])
        acc = None
        r0 = 0
        for s in scores:
            p = jnp.exp((s - m).astype(BF16))
            part = _dot(p, vext[r0:r0 + s.shape[1], :])
            acc = part if acc is None else acc + part
            r0 += s.shape[1]
        outs.append(acc[:, :ATTN_DV] / acc[:, ATTN_DV:])
    o = outs[0] - lam_ref[0] * outs[1]
    o = o * lax.rsqrt(jnp.mean(o * o, axis=-1, keepdims=True) + RMS_EPS) * g_ref[...]
    o_ref[...] = (o * (1.0 - lam_init)).astype(o_ref.dtype)


def diff_attention(zq, key_srcs, lam, subln_g, lam_init, batch):
    lq = zq.shape[0] // batch
    tq = min(256, lq)
    nq = lq // tq
    lks = [k.shape[0] // batch for k in key_srcs]
    n_src = len(key_srcs)
    hpb = D_MODEL // LANES
    in_specs = [pl.BlockSpec(memory_space=pltpu.SMEM),
                pl.BlockSpec((1, ATTN_DV), lambda b, h, i: (0, 0)),
                pl.BlockSpec((tq, LANES), lambda b, h, i: (b * nq + i, COL_Q * hpb + h))]
    in_specs += [pl.BlockSpec((lk, LANES), lambda b, h, i: (b, COL_K * hpb + h)) for lk in lks]
    in_specs += [pl.BlockSpec((lk, LANES), lambda b, h, i: (b, COL_V * hpb + h)) for lk in lks]
    return pl.pallas_call(
        functools.partial(_attn_kernel, n_src=n_src, lam_init=lam_init),
        out_shape=jax.ShapeDtypeStruct((batch * lq, D_MODEL), BF16),
        grid=(batch, ATTN_HEADS, nq),
        in_specs=in_specs,
        out_specs=pl.BlockSpec((tq, LANES), lambda b, h, i: (b * nq + i, h)),
        scratch_shapes=[pltpu.VMEM((sum(lks), 2 * ATTN_DV), BF16)],
        compiler_params=_cparams(("parallel", "parallel", "arbitrary")),
        name="diff_attention",
    )(lam.reshape(1), subln_g.reshape(1, ATTN_DV), zq, *key_srcs, *key_srcs)


def _merge_kernel(yf_ref, yr_ref, u_ref, ac_ref, aa_ref, zg0_ref, zg1_ref, zg2_ref, x_ref,
                  d_ref, wglu_ref, ws_ref, wc_ref, wa_ref, wo_ref, g1_ref, sc2_ref, sh2_ref,
                  lng_ref, lnb_ref, wr_hi_ref, wr_lo_ref, x1_ref, h2_ref, lg_ref, *, alpha):
    y = yf_ref[...] + yr_ref[...] + d_ref[...] * u_ref[...].astype(F32)
    g = 0.5 * y * (1.0 + jnp.tanh(math.sqrt(2.0 / math.pi) * (y + 0.044715 * (y * y * y))))
    a_ssm = g * _sigmoid(_dot(g.astype(BF16), wglu_ref[...]))
    y_ssm = _dot(a_ssm.astype(BF16), ws_ref[...])
    y_conv = _dot(ac_ref[...], wc_ref[...])
    y_attn = _dot(aa_ref[...], wa_ref[...])
    m = (_sigmoid(zg0_ref[...].astype(F32)) * y_conv + _sigmoid(zg1_ref[...].astype(F32)) * y_ssm
         + _sigmoid(zg2_ref[...].astype(F32)) * y_attn)
    yo = _dot(m.astype(BF16), wo_ref[...])
    x1 = _ln(alpha * x_ref[...] + g1_ref[...] * yo) * lng_ref[...] + lnb_ref[...]
    x1_ref[...] = x1
    h2 = _ln(x1) * (1.0 + sc2_ref[...]) + sh2_ref[...]
    h2_ref[...] = h2.astype(BF16)
    hi, lo = _split_bf16(h2)
    lg_ref[...] = _dot_nt(wr_hi_ref[...], hi) + _dot_nt(wr_hi_ref[...], lo) + _dot_nt(wr_lo_ref[...], hi)


def merge_residual(yf, yr, z, a_conv, a_attn, x2d, mods, wts, seq_len, alpha):
    m = x2d.shape[0]
    batch = m // seq_len
    tm = min(512, seq_len)
    tps = seq_len // tm
    g1, sc2, sh2 = mods
    nb = g1.shape[0]
    mod_map = (lambda i: (i // tps, 0, 0)) if nb > 1 else (lambda i: (0, 0, 0))
    tile = pl.BlockSpec((tm, D_MODEL), lambda i: (i, 0))
    zcol = lambda cidx: pl.BlockSpec((tm, D_MODEL), lambda i: (i, cidx))
    modspec = pl.BlockSpec((None, 1, D_MODEL), mod_map)
    rowspec = pl.BlockSpec((1, D_MODEL), lambda i: (0, 0))
    wspec = pl.BlockSpec((D_MODEL, D_MODEL), lambda i: (0, 0))
    wrspec = pl.BlockSpec((N_EXPERTS, D_MODEL), lambda i: (0, 0))
    return pl.pallas_call(
        functools.partial(_merge_kernel, alpha=alpha),
        out_shape=(jax.ShapeDtypeStruct((m, D_MODEL), F32), jax.ShapeDtypeStruct((m, D_MODEL), BF16),
                   jax.ShapeDtypeStruct((batch, N_EXPERTS, seq_len), F32)),
        grid=(m // tm,),
        in_specs=[tile, tile, zcol(COL_U), tile, tile, zcol(COL_GATE0), zcol(COL_GATE0 + 1), zcol(COL_GATE0 + 2),
                  tile, rowspec, wspec, wspec, wspec, wspec, wspec, modspec, modspec, modspec,
                  rowspec, rowspec, wrspec, wrspec],
        out_specs=(tile, tile, pl.BlockSpec((None, N_EXPERTS, tm), lambda i: (i // tps, 0, i % tps))),
        compiler_params=_cparams(("parallel",)),
        name="merge_residual",
    )(yf, yr, z, a_conv, a_attn, z, z, z, x2d, wts['ssm_d'], wts['w_ssm_glu'], wts['w_ssm_out'],
      wts['w_conv_out'], wts['w_attn_out'], wts['w_o'], g1, sc2, sh2, wts['ln1_g'], wts['ln1_b'],
      wts['wr_hi'], wts['wr_lo'])


def _select_kernel(lg_ref, tri_ref, sel_ref, aff_ref, csum, *, cap):
    lg = lg_ref[...]
    n = lg.shape[1]
    e = jnp.exp(lg - jnp.max(lg, axis=0, keepdims=True))
    aff = e / jnp.sum(e, axis=0, keepdims=True)
    aff_ref[...] = aff
    bits = pltpu.bitcast(aff, jnp.int32)
    capf = float(cap)

    def count(mask):
        return jnp.sum(jnp.where(mask, 1.0, 0.0), axis=1, keepdims=True)

    thr = jnp.zeros((lg.shape[0], 1), jnp.int32)
    for bit in range(30, -1, -1):
        cand = thr | (1 << bit)
        thr = jnp.where(count(bits >= cand) >= capf, cand, thr)
    gt = bits > thr
    eq = bits == thr
    need = capf - count(gt)

    def excl_cumsum(mask):
        x = jnp.where(mask, 1.0, 0.0)
        carry = jnp.zeros((lg.shape[0], 1), F32)
        for j in range(n // LANES):
            blk = x[:, j * LANES:(j + 1) * LANES]
            inc = _dot(blk.astype(BF16), tri_ref[...])
            csum[:, j * LANES:(j + 1) * LANES] = inc - blk + carry
            carry = carry + inc[:, LANES - 1:LANES]
        return csum[...]

    mask = gt | (eq & (excl_cumsum(eq) < need))
    pos = excl_cumsum(mask)
    sel_ref[...] = jnp.where(mask, pos, -1.0)


def route_select(logits_t, cap):
    batch, n_e, n = logits_t.shape
    tri = jnp.asarray(np.triu(np.ones((LANES, LANES), np.float32)), BF16)
    spec = pl.BlockSpec((None, n_e, n), lambda b: (b, 0, 0))
    shp = jax.ShapeDtypeStruct((batch, n_e, n), F32)
    return pl.pallas_call(
        functools.partial(_select_kernel, cap=cap),
        out_shape=(shp, shp),
        grid=(batch,),
        in_specs=[spec, pl.BlockSpec((LANES, LANES), lambda b: (0, 0))],
        out_specs=(spec, spec),
        scratch_shapes=[pltpu.VMEM((n_e, n), F32)],
        compiler_params=_cparams(("parallel",)),
        name="route_select",
    )(logits_t, tri)


def _gather_kernel(sel_ref, h_ref, xs_ref):
    cap = xs_ref.shape[0]
    slot = lax.broadcasted_iota(jnp.int32, (cap, 1), 0).astype(F32)
    onehot = jnp.where(sel_ref[...] == slot, 1.0, 0.0).astype(BF16)
    xs_ref[...] = _dot(onehot, h_ref[...]).astype(xs_ref.dtype)


def gather_tokens(sel4, h2, cap):
    batch, n_e, _, n = sel4.shape
    return pl.pallas_call(
        _gather_kernel,
        out_shape=jax.ShapeDtypeStruct((n_e, batch * cap, D_MODEL), BF16),
        grid=(batch, n_e),
        in_specs=[pl.BlockSpec((None, None, 1, n), lambda b, e: (b, e, 0, 0)),
                  pl.BlockSpec((n, D_MODEL), lambda b, e: (b, 0))],
        out_specs=pl.BlockSpec((None, cap, D_MODEL), lambda b, e: (e, b, 0)),
        compiler_params=_cparams(("parallel", "arbitrary")),
        name="gather_tokens",
    )(sel4, h2)


def _experts_kernel(*refs, n_grp, caps, batch):
    wg_ref, wu_ref, wd_ref = refs[:3]
    xs_refs = refs[3:3 + n_grp]
    sel_refs = refs[3 + n_grp:3 + 2 * n_grp]
    aff_refs = refs[3 + 2 * n_grp:3 + 3 * n_grp]
    ye_refs = refs[3 + 3 * n_grp:3 + 4 * n_grp]
    acc_refs = refs[3 + 4 * n_grp:]
    f = pl.program_id(1)
    nf = pl.num_programs(1)
    wg = wg_ref[...].astype(BF16)
    wu = wu_ref[...].astype(BF16)
    wd = wd_ref[...].astype(BF16)
    for xs_ref, acc_ref in zip(xs_refs, acc_refs):
        rows = xs_ref.shape[0]
        step = min(512, rows)
        for r0 in range(0, rows, step):
            x = xs_ref[r0:r0 + step, :]
            a = _dot(x, wg)
            act = (a * _sigmoid(a) * _dot(x, wu)).astype(BF16)
            part = _dot(act, wd)

            @pl.when(f == 0)
            def _():
                acc_ref[r0:r0 + step, :] = part

            @pl.when(f > 0)
            def _():
                acc_ref[r0:r0 + step, :] += part

    @pl.when(f == nf - 1)
    def _():
        for sel_ref, aff_ref, ye_ref, acc_ref, cap in zip(sel_refs, aff_refs, ye_refs, acc_refs, caps):
            slot = lax.broadcasted_iota(jnp.int32, (cap, 1), 0).astype(F32)
            for b in range(batch):
                gate = jnp.sum(jnp.where(sel_ref[b] == slot, aff_ref[b], 0.0), axis=1, keepdims=True)
                rs = slice(b * cap, (b + 1) * cap)
                ye_ref[rs, :] = (acc_ref[rs, :] * gate).astype(ye_ref.dtype)


def expert_ffn(groups, w_gate_up, w_down, batch):
    tf = 256
    nf = D_EXPERT // tf
    n_grp = len(groups)
    caps = tuple(g[3] for g in groups)
    xs_specs = [pl.BlockSpec((None, g[0].shape[1], D_MODEL), lambda e, f: (e, 0, 0)) for g in groups]
    sa_specs = [pl.BlockSpec((batch, None, 1, g[1].shape[3]), lambda e, f: (0, e, 0, 0)) for g in groups]
    return pl.pallas_call(
        functools.partial(_experts_kernel, n_grp=n_grp, caps=caps, batch=batch),
        out_shape=tuple(jax.ShapeDtypeStruct(g[0].shape, BF16) for g in groups),
        grid=(N_EXPERTS, nf),
        in_specs=[pl.BlockSpec((None, D_MODEL, tf), lambda e, f: (e, 0, f)),
                  pl.BlockSpec((None, D_MODEL, tf), lambda e, f: (e, 0, nf + f)),
                  pl.BlockSpec((None, tf, D_MODEL), lambda e, f: (e, f, 0))] + xs_specs + sa_specs + sa_specs,
        out_specs=tuple(xs_specs),
        scratch_shapes=[pltpu.VMEM(g[0].shape[1:], F32) for g in groups],
        compiler_params=_cparams(("parallel", "arbitrary")),
        name="expert_ffn",
    )(w_gate_up, w_gate_up, w_down, *[g[0] for g in groups], *[g[1] for g in groups], *[g[2] for g in groups])


def _combine_kernel(selt_ref, ye_ref, x1_ref, g2_ref, lng_ref, lnb_ref, o_ref, acc_ref, *, alpha):
    e = pl.program_id(2)
    cap = ye_ref.shape[0]
    st = selt_ref[...]
    lane = lax.broadcasted_iota(jnp.int32, st.shape, 1)
    col = jnp.sum(jnp.where(lane == e, st, 0.0), axis=1, keepdims=True)
    slot = lax.broadcasted_iota(jnp.int32, (1, cap), 1).astype(F32)
    onehot = jnp.where(col == slot, 1.0, 0.0).astype(BF16)
    part = _dot(onehot, ye_ref[...])

    @pl.when(e == 0)
    def _():
        acc_ref[...] = part

    @pl.when(e > 0)
    def _():
        acc_ref[...] += part

    @pl.when(e == pl.num_programs(2) - 1)
    def _():
        o_ref[...] = _ln(alpha * x1_ref[...] + g2_ref[...] * acc_ref[...]) * lng_ref[...] + lnb_ref[...]


def combine_residual(sel_t, ye, x1, g2, ln_g, ln_b, cap, alpha):
    batch, n, n_e = sel_t.shape
    tn = min(1024, n)
    nt = n // tn
    nb = g2.shape[0]
    mod_map = (lambda b, t, e: (b, 0, 0)) if nb > 1 else (lambda b, t, e: (0, 0, 0))
    rowspec = pl.BlockSpec((1, D_MODEL), lambda b, t, e: (0, 0))
    tile = pl.BlockSpec((tn, D_MODEL), lambda b, t, e: (b * nt + t, 0))
    return pl.pallas_call(
        functools.partial(_combine_kernel, alpha=alpha),
        out_shape=jax.ShapeDtypeStruct((batch * n, D_MODEL), F32),
        grid=(batch, nt, n_e),
        in_specs=[pl.BlockSpec((None, tn, n_e), lambda b, t, e: (b, t, 0)),
                  pl.BlockSpec((None, cap, D_MODEL), lambda b, t, e: (e, b, 0)),
                  tile, pl.BlockSpec((None, 1, D_MODEL), mod_map), rowspec, rowspec],
        out_specs=tile,
        scratch_shapes=[pltpu.VMEM((tn, D_MODEL), F32)],
        compiler_params=_cparams(("parallel", "parallel", "arbitrary")),
        name="combine_residual",
    )(sel_t, ye, x1, g2, ln_g, ln_b)


def kernel(x, c, ctx, c_ctx, w_ada, b_ada, w_in, conv_w, conv_b, conv_ln_g, conv_ln_b, w_conv_out, ssm_a_re, ssm_a_im, ssm_log_dt, ssm_b_re, ssm_b_im, ssm_c_re, ssm_c_im, ssm_d, w_ssm_glu, w_ssm_out, attn_lambda, attn_subln_g, w_attn_out, w_o, ln1_g, ln1_b, w_router, w_gate_up, w_down, ln2_g, ln2_b):
    batch, seq_len, d = x.shape
    ctx_len = ctx.shape[1]
    depth = w_in.shape[0]
    alpha = (2.0 * depth) ** 0.25
    cap = CAPACITY_FACTOR * seq_len // N_EXPERTS
    cap_c = CAPACITY_FACTOR * ctx_len // N_EXPERTS
    tabs = rope_tables(seq_len)
    tabs_c = tuple(t[:ctx_len] for t in tabs)
    n_state = 2 * SSM_GROUPS * SSM_STATE
    row = lambda a: a.reshape(1, d)
    cc = jnp.zeros((8, d), F32).at[:batch].set(c).at[batch].set(c_ctx)

    x2 = x.reshape(batch * seq_len, d)
    xc2 = ctx.reshape(batch * ctx_len, d)
    for l in range(depth):
        ctx_out = l < depth - 1
        lam_init = 0.8 - 0.6 * math.exp(-0.3 * l)
        mod = ada_modulation(cc, w_ada[l], b_ada[l])
        part = lambda rows, k: mod[rows, k * d:(k + 1) * d][:, None, :]
        lat = [part(slice(0, batch), k) for k in range(6)]
        cxm = [part(slice(batch, batch + 1), k) for k in range(6)]
        w_in_b = w_in[l].astype(BF16)
        wr_hi, wr_lo = _split_bf16(w_router[l].T)
        wts = {'ssm_d': row(ssm_d[l]), 'w_ssm_glu': w_ssm_glu[l].astype(BF16), 'w_ssm_out': w_ssm_out[l].astype(BF16),
               'w_conv_out': w_conv_out[l].astype(BF16), 'w_attn_out': w_attn_out[l].astype(BF16),
               'w_o': w_o[l].astype(BF16), 'ln1_g': row(ln1_g[l]), 'ln1_b': row(ln1_b[l]),
               'wr_hi': wr_hi, 'wr_lo': wr_lo}
        lq1, lk1, lq2, lk2 = [attn_lambda[l, i].astype(F32) for i in range(4)]
        lam = jnp.exp(jnp.sum(lq1 * lk1)) - jnp.exp(jnp.sum(lq2 * lk2)) + lam_init
        a_rows, bw, cw = ssm_prepare(ssm_a_re[l], ssm_a_im[l], ssm_log_dt[l], ssm_b_re[l], ssm_b_im[l],
                                     ssm_c_re[l], ssm_c_im[l], batch)

        z = input_projection(x2, lat[1], lat[0], w_in_b, tabs, seq_len, rope=True)
        zc = input_projection(xc2, cxm[1], cxm[0], w_in_b, tabs_c, ctx_len, rope=False)

        a_conv = conv_branch(z, seq_len, conv_w[l], conv_b[l], conv_ln_g[l], conv_ln_b[l])
        h_zero = jnp.zeros((2 * batch, n_state), F32)
        ycf, ycr, h_ctx = ssm_scan(zc.reshape(batch, ctx_len, -1), h_zero, a_rows, bw, cw, write_y=ctx_out)
        yf, yr, _ = ssm_scan(z.reshape(batch, seq_len, -1), h_ctx, a_rows, bw, cw, write_y=True)
        a_attn = diff_attention(z, [zc, z], lam, attn_subln_g[l], lam_init, batch)
        x1, h2, lg = merge_residual(yf.reshape(-1, d), yr.reshape(-1, d), z, a_conv, a_attn, x2,
                                    (lat[2], lat[4], lat[3]), wts, seq_len, alpha)
        sel, aff = route_select(lg, cap)
        sel4, aff4 = sel[:, :, None, :], aff[:, :, None, :]
        groups = [(gather_tokens(sel4, h2, cap), sel4, aff4, cap)]
        if ctx_out:
            a_conv_c = conv_branch(zc, ctx_len, conv_w[l], conv_b[l], conv_ln_g[l], conv_ln_b[l])
            a_attn_c = diff_attention(zc, [zc], lam, attn_subln_g[l], lam_init, batch)
            xc1, hc2, lgc = merge_residual(ycf.reshape(-1, d), ycr.reshape(-1, d), zc, a_conv_c, a_attn_c, xc2,
                                           (cxm[2], cxm[4], cxm[3]), wts, ctx_len, alpha)
            selc, affc = route_select(lgc, cap_c)
            selc4, affc4 = selc[:, :, None, :], affc[:, :, None, :]
            groups.append((gather_tokens(selc4, hc2, cap_c), selc4, affc4, cap_c))
        ye = expert_ffn(groups, w_gate_up[l], w_down[l], batch)
        x2 = combine_residual(jnp.swapaxes(sel, 1, 2), ye[0], x1, lat[5], row(ln2_g[l]), row(ln2_b[l]), cap, alpha)
        if ctx_out:
            xc2 = combine_residual(jnp.swapaxes(selc, 1, 2), ye[1], xc1, cxm[5], row(ln2_g[l]), row(ln2_b[l]),
                                   cap_c, alpha)
    return x2.reshape(batch, seq_len, d)
```

```python
import functools
import math

import jax
import jax.numpy as jnp
import numpy as np
from jax import lax
from jax.experimental import pallas as pl
from jax.experimental.pallas import tpu as pltpu

F32 = jnp.float32
BF16 = jnp.bfloat16

D_MODEL = 1024
GRID_W = 64
N_BRANCH = 3
CONV_WIDTH = 31
CONV_HALO = 16
SSM_GROUP = 16
SSM_GROUPS = D_MODEL // SSM_GROUP
SSM_STATE = 64
SSM_CHUNK = 16
SSM_ROWS = 8
GROUP_W = SSM_CHUNK * SSM_GROUP
SSM_GB = 2
ATTN_HEADS = 8
ATTN_DH = 64
ATTN_DV = 128
ROPE_BASE = 10000.0
N_EXPERTS = 16
D_EXPERT = 2816
CAPACITY_FACTOR = 2
LN_EPS = 1e-6
RMS_EPS = 1e-5
LANES = 128

COL_CONV_V, COL_CONV_G, COL_GATE0, COL_Q, COL_U, COL_K, COL_V = 0, 1, 2, 5, 6, 7, 8
N_COLS = 9

VMEM_LIMIT = 56 * 1024 * 1024


def _cparams(sem, vmem=VMEM_LIMIT):
    return pltpu.CompilerParams(dimension_semantics=sem, vmem_limit_bytes=vmem)


def _ln(x):
    mu = jnp.mean(x, axis=-1, keepdims=True)
    xc = x - mu
    var = jnp.mean(xc * xc, axis=-1, keepdims=True)
    return xc * lax.rsqrt(var + LN_EPS)


def _sigmoid(x):
    return 0.5 * jnp.tanh(0.5 * x) + 0.5


def _dot(a, b):
    return jnp.dot(a, b, preferred_element_type=F32)


def _dot_nt(a, b):
    return lax.dot_general(a, b, (((1,), (1,)), ((), ())), preferred_element_type=F32)


def _split_bf16(x):
    hi = x.astype(BF16)
    lo = (x - hi.astype(F32)).astype(BF16)
    return hi, lo


def _ada_kernel(c_ref, w_ref, b_ref, o_ref):
    c = c_ref[...]
    s = c * _sigmoid(c)
    hi, lo = _split_bf16(s)
    whi, wlo = _split_bf16(w_ref[...])
    o_ref[...] = _dot(hi, whi) + _dot(hi, wlo) + _dot(lo, whi) + b_ref[...]


def ada_modulation(cc, w_ada, b_ada):
    n = w_ada.shape[1]
    tn = 1024
    return pl.pallas_call(
        _ada_kernel,
        out_shape=jax.ShapeDtypeStruct((8, n), F32),
        grid=(n // tn,),
        in_specs=[pl.BlockSpec((8, D_MODEL), lambda j: (0, 0)),
                  pl.BlockSpec((D_MODEL, tn), lambda j: (0, j)),
                  pl.BlockSpec((1, tn), lambda j: (0, j))],
        out_specs=pl.BlockSpec((8, tn), lambda j: (0, j)),
        compiler_params=_cparams(("arbitrary",)),
        name="ada_modulation",
    )(cc, w_ada, b_ada.reshape(1, n))


def _inproj_kernel(x_ref, sc_ref, sh_ref, w_ref, cos_ref, sa_ref, sb_ref, o_ref, h_ref, *, rope):
    j = pl.program_id(1)

    @pl.when(j == 0)
    def _():
        h = _ln(x_ref[...]) * (1.0 + sc_ref[...]) + sh_ref[...]
        h_ref[...] = h.astype(BF16)

    acc = _dot(h_ref[...], w_ref[...])
    tn = acc.shape[1]

    def roped(scale):
        cos, sa, sb = cos_ref[...], sa_ref[...], sb_ref[...]
        for hh in range(tn // LANES):
            seg = acc[:, hh * LANES:(hh + 1) * LANES]
            r = seg * cos + pltpu.roll(seg, LANES - 16, 1) * sa + pltpu.roll(seg, 16, 1) * sb
            o_ref[:, hh * LANES:(hh + 1) * LANES] = (r * scale).astype(o_ref.dtype)

    is_q = j == COL_Q
    is_k = j == COL_K
    if rope:
        @pl.when(is_q)
        def _():
            roped(ATTN_DH ** -0.5)

        @pl.when(is_k)
        def _():
            roped(1.0)

        @pl.when(jnp.logical_not(is_q | is_k))
        def _():
            o_ref[...] = acc.astype(o_ref.dtype)
    else:
        @pl.when(is_q)
        def _():
            o_ref[...] = (acc * (ATTN_DH ** -0.5)).astype(o_ref.dtype)

        @pl.when(jnp.logical_not(is_q))
        def _():
            o_ref[...] = acc.astype(o_ref.dtype)


def input_projection(x2d, sc, sh, w_bf16, rope_tabs, seq_len, rope):
    m = x2d.shape[0]
    tm = min(1024, seq_len)
    tps = seq_len // tm
    nb = sc.shape[0]
    cos, sa, sb = rope_tabs
    mod_map = (lambda i, j: (i // tps, 0, 0)) if nb > 1 else (lambda i, j: (0, 0, 0))
    tab_map = lambda i, j: (i % tps, 0)
    return pl.pallas_call(
        functools.partial(_inproj_kernel, rope=rope),
        out_shape=jax.ShapeDtypeStruct((m, N_COLS * D_MODEL), BF16),
        grid=(m // tm, N_COLS),
        in_specs=[pl.BlockSpec((tm, D_MODEL), lambda i, j: (i, 0)),
                  pl.BlockSpec((None, 1, D_MODEL), mod_map),
                  pl.BlockSpec((None, 1, D_MODEL), mod_map),
                  pl.BlockSpec((D_MODEL, D_MODEL), lambda i, j: (0, j)),
                  pl.BlockSpec((tm, LANES), tab_map),
                  pl.BlockSpec((tm, LANES), tab_map),
                  pl.BlockSpec((tm, LANES), tab_map)],
        out_specs=pl.BlockSpec((tm, D_MODEL), lambda i, j: (i, j)),
        scratch_shapes=[pltpu.VMEM((tm, D_MODEL), BF16)],
        compiler_params=_cparams(("parallel", "arbitrary")),
        name="input_projection",
    )(x2d, sc, sh, w_bf16, cos, sa, sb)


def rope_tables(length):
    rows = length // GRID_W
    row = jnp.repeat(jnp.arange(rows), GRID_W)
    col = jnp.tile(jnp.arange(GRID_W), rows)
    n_freq = ATTN_DH // 4
    inv_freq = ROPE_BASE ** (-jnp.arange(n_freq, dtype=F32) / n_freq)
    ang = jnp.stack([row, col], -1).astype(F32)[:, :, None, None] * inv_freq
    ang = jnp.broadcast_to(ang, (length, 2, 2, n_freq)).reshape(length, ATTN_DH)
    cos, sin = jnp.cos(ang), jnp.sin(ang)
    first_half = (jnp.arange(ATTN_DH) % 32) < 16
    sa = jnp.where(first_half, -sin, 0.0)
    sb = jnp.where(first_half, 0.0, sin)
    tile2 = lambda t: jnp.concatenate([t, t], axis=-1)
    return tile2(cos), tile2(sa), tile2(sb)


def _conv_kernel(cur_ref, prev_ref, next_ref, w_ref, b_ref, g_ref, beta_ref, o_ref, gbuf, ybuf, *, tps):
    ti = pl.program_id(0) % tps
    tm = cur_ref.shape[0]
    d = D_MODEL

    def glu(ref):
        return ref[:, :d].astype(F32) * _sigmoid(ref[:, d:].astype(F32))

    gbuf[CONV_HALO:CONV_HALO + tm, :] = glu(cur_ref)
    gbuf[0:CONV_HALO, :] = jnp.where(ti > 0, glu(prev_ref), 0.0)
    gbuf[CONV_HALO + tm:, :] = jnp.where(ti < tps - 1, glu(next_ref), 0.0)
    off = CONV_HALO - CONV_WIDTH // 2
    rows = min(tm, 128)
    for c in range(d // LANES):
        cs = slice(c * LANES, (c + 1) * LANES)
        for r0 in range(0, tm, rows):
            acc = None
            for res in range(8):
                part = None
                for o in range(res, off + CONV_WIDTH, 8):
                    if o < off:
                        continue
                    term = gbuf[r0 + o - res:r0 + o - res + rows + 8, cs] * w_ref[o - off:o - off + 1, cs]
                    part = term if part is None else part + term
                part = part[res:res + rows, :]
                acc = part if acc is None else acc + part
            ybuf[r0:r0 + rows, cs] = acc
    y = _ln(ybuf[...] + b_ref[...]) * g_ref[...] + beta_ref[...]
    o_ref[...] = (y * _sigmoid(y)).astype(o_ref.dtype)


def conv_branch(z, seq_len, conv_w, conv_b, ln_g, ln_b):
    m = z.shape[0]
    tm = min(256, seq_len)
    tps = seq_len // tm
    hb = tm // CONV_HALO
    nblk = m // CONV_HALO
    row = lambda a: a.reshape(1, D_MODEL)
    return pl.pallas_call(
        functools.partial(_conv_kernel, tps=tps),
        out_shape=jax.ShapeDtypeStruct((m, D_MODEL), BF16),
        grid=(m // tm,),
        in_specs=[pl.BlockSpec((tm, 2 * D_MODEL), lambda i: (i, 0)),
                  pl.BlockSpec((CONV_HALO, 2 * D_MODEL), lambda i: (jnp.maximum(i * hb - 1, 0), 0)),
                  pl.BlockSpec((CONV_HALO, 2 * D_MODEL), lambda i: (jnp.minimum((i + 1) * hb, nblk - 1), 0)),
                  pl.BlockSpec((CONV_WIDTH, D_MODEL), lambda i: (0, 0)),
                  pl.BlockSpec((1, D_MODEL), lambda i: (0, 0)),
                  pl.BlockSpec((1, D_MODEL), lambda i: (0, 0)),
                  pl.BlockSpec((1, D_MODEL), lambda i: (0, 0))],
        out_specs=pl.BlockSpec((tm, D_MODEL), lambda i: (i, 0)),
        scratch_shapes=[pltpu.VMEM((tm + 2 * CONV_HALO, D_MODEL), F32), pltpu.VMEM((tm, D_MODEL), F32)],
        compiler_params=_cparams(("parallel",)),
        name="conv_branch",
    )(z, z, z, conv_w, row(conv_b), row(ln_g), row(ln_b))


def ssm_prepare(a_re, a_im, log_dt, b_re, b_im, c_re, c_im):
    a = lax.complex(a_re.astype(F32), a_im.astype(F32))
    dt_a = jnp.exp(log_dt.astype(F32))[..., None] * a
    b = lax.complex(b_re.astype(F32), b_im.astype(F32))
    b_bar = ((jnp.exp(dt_a) - 1.0) / a)[..., None] * b
    cc = lax.complex(c_re.astype(F32), c_im.astype(F32))
    n = SSM_CHUNK
    steps = jnp.arange(n + 1, dtype=F32)
    pw = jnp.exp(steps[None, None, :, None] * dt_a[:, :, None, :])
    lag_resp = jnp.real(jnp.einsum('dgop,dgkp,dgpi->dgkio', cc, pw[:, :, :n], b_bar))
    idx = jnp.arange(n)

    def toeplitz(resp, lag):
        t = resp[:, jnp.clip(lag, 0, n - 1)]
        t = jnp.where((lag >= 0)[None, :, :, None, None], t, 0.0)
        return t.transpose(0, 1, 3, 2, 4).reshape(SSM_GROUPS, GROUP_W, GROUP_W)

    t_sum = toeplitz(lag_resp[0], idx[None, :] - idx[:, None]) + toeplitz(lag_resp[1], idx[:, None] - idx[None, :])

    def parts(z, sign):
        return [jnp.real(z), sign * jnp.imag(z)]

    w_f = jnp.einsum('gsp,gpi->gsip', pw[0][:, n - 1 - idx], b_bar[0]).reshape(SSM_GROUPS, GROUP_W, SSM_STATE)
    w_r = jnp.einsum('gsp,gpi->gsip', pw[1][:, idx], b_bar[1]).reshape(SSM_GROUPS, GROUP_W, SSM_STATE)
    w_cat = jnp.concatenate(parts(w_f, 1.0) + parts(w_r, 1.0), axis=-1)
    v_f = jnp.einsum('gop,gtp->gpto', cc[0], pw[0][:, 1 + idx]).reshape(SSM_GROUPS, SSM_STATE, GROUP_W)
    v_r = jnp.einsum('gop,gtp->gpto', cc[1], pw[1][:, n - idx]).reshape(SSM_GROUPS, SSM_STATE, GROUP_W)
    v_cat = jnp.concatenate(parts(v_f, -1.0) + parts(v_r, -1.0), axis=1)
    a_n = pw[:, :, n]
    mul_same = jnp.concatenate([jnp.real(a_n[0])] * 2 + [jnp.real(a_n[1])] * 2, axis=-1)
    mul_swap = jnp.concatenate([-jnp.imag(a_n[0]), jnp.imag(a_n[0]), -jnp.imag(a_n[1]), jnp.imag(a_n[1])],
                               axis=-1)
    flat = lambda t: t.reshape(1, SSM_GROUPS * GROUP_W)
    return _split_bf16(t_sum) + _split_bf16(w_cat) + _split_bf16(v_cat) + (flat(mul_same), flat(mul_swap))


def ssm_chunk_layout(u):
    b, t, _ = u.shape
    nch = t // SSM_CHUNK
    u5 = u.reshape(b, nch, SSM_CHUNK, SSM_GROUPS, SSM_GROUP).transpose(1, 0, 3, 2, 4)
    u5 = jnp.pad(u5, ((0, 0), (0, SSM_ROWS - b), (0, 0), (0, 0), (0, 0)))
    return u5.reshape(nch * SSM_ROWS, SSM_GROUPS * GROUP_W)


def ssm_token_layout(y2, batch):
    nch = y2.shape[0] // SSM_ROWS
    y5 = y2.reshape(nch, SSM_ROWS, SSM_GROUPS, SSM_CHUNK, SSM_GROUP)[:, :batch]
    return y5.transpose(1, 0, 3, 2, 4).reshape(batch, nch * SSM_CHUNK, D_MODEL)


def _ssm_kernel(u_ref, t_hi_ref, t_lo_ref, w_hi_ref, w_lo_ref, v_hi_ref, v_lo_ref, same_ref, swap_ref,
                y_ref, sbuf, xbuf, *, n_ctx):
    nch, rows, width = sbuf.shape
    groups = t_hi_ref.shape[0]
    gsl = lambda k: slice(k * GROUP_W, (k + 1) * GROUP_W)
    cols = [slice(j * LANES, (j + 1) * LANES) for j in range(width // LANES)]
    for k in range(groups):
        u = u_ref[:, gsl(k)]
        inc = _dot(u, w_hi_ref[k]) + _dot(u, w_lo_ref[k])
        sbuf[:, :, gsl(k)] = inc.reshape(nch, rows, GROUP_W)
        for cs in cols[2 * k:2 * k + 2]:
            twin = pltpu.roll(inc[:, cs.start - k * GROUP_W:cs.stop - k * GROUP_W], LANES // 2, 1)
            xbuf[:, :, cs] = twin.reshape(nch, rows, LANES)

    same = jnp.broadcast_to(same_ref[...], (rows, width))
    swap = jnp.broadcast_to(swap_ref[...], (rows, width))

    def body(j, state):
        cr = jnp.where(j < n_ctx, n_ctx - 1 - j, nch - 1 - (j - n_ctx))
        new = []
        for v, cs in enumerate(cols):
            ci = j if v % 2 == 0 else cr
            s, sx = state[2 * v], state[2 * v + 1]
            inc = sbuf[ci, :, cs]
            sbuf[ci, :, cs] = s
            new.append(s * same[:, cs] + sx * swap[:, cs] + inc)
            new.append(sx * same[:, cs] - s * swap[:, cs] + xbuf[ci, :, cs])
        return tuple(new)

    lax.fori_loop(0, nch, body, tuple(jnp.zeros((rows, LANES), F32) for _ in range(2 * len(cols))), unroll=4)

    for k in range(groups):
        u = u_ref[:, gsl(k)]
        s_hi, s_lo = _split_bf16(sbuf[:, :, gsl(k)].reshape(nch * rows, GROUP_W))
        y_ref[:, gsl(k)] = (_dot(u, t_hi_ref[k]) + _dot(u, t_lo_ref[k]) + _dot(s_hi, v_hi_ref[k])
                            + _dot(s_hi, v_lo_ref[k]) + _dot(s_lo, v_hi_ref[k]))


def ssm_mixer(u2, ops, n_ctx_chunks):
    t_hi, t_lo, w_hi, w_lo, v_hi, v_lo, mul_same, mul_swap = ops
    rows, width = u2.shape
    nch = rows // SSM_ROWS
    gw = SSM_GB * GROUP_W
    col = pl.BlockSpec((rows, gw), lambda g: (0, g))
    wsp = pl.BlockSpec((SSM_GB, GROUP_W, GROUP_W), lambda g: (g, 0, 0))
    tab = pl.BlockSpec((1, gw), lambda g: (0, g))
    return pl.pallas_call(
        functools.partial(_ssm_kernel, n_ctx=n_ctx_chunks),
        out_shape=jax.ShapeDtypeStruct((rows, width), F32),
        grid=(SSM_GROUPS // SSM_GB,),
        in_specs=[col] + [wsp] * 6 + [tab, tab],
        out_specs=col,
        scratch_shapes=[pltpu.VMEM((nch, SSM_ROWS, gw), F32), pltpu.VMEM((nch, SSM_ROWS, gw), F32)],
        compiler_params=_cparams(("parallel",)),
        name="ssm_mixer",
    )(u2, t_hi, t_lo, w_hi, w_lo, v_hi, v_lo, mul_same, mul_swap)


def _attn_kernel(*refs, n_src, lam_init, key_block):
    lam_ref, g_ref, q_ref = refs[:3]
    k_refs = refs[3:3 + n_src]
    v_refs = refs[3 + n_src:3 + 2 * n_src]
    o_ref, vext = refs[3 + 2 * n_src], refs[4 + 2 * n_src]
    qi = pl.program_id(2)

    @pl.when(qi == 0)
    def _():
        r0 = 0
        for v_ref in v_refs:
            n = v_ref.shape[0]
            vext[r0:r0 + n, :ATTN_DV] = v_ref[...]
            vext[r0:r0 + n, ATTN_DV:] = jnp.ones((n, ATTN_DV), BF16)
            r0 += n

    q = q_ref[...]
    lane = lax.broadcasted_iota(jnp.int32, q.shape, 1)
    outs = []
    for sub in range(2):
        qs = jnp.where((lane >= ATTN_DH) if sub else (lane < ATTN_DH), q, jnp.zeros_like(q))
        m = acc = None
        r0 = 0
        for k_ref in k_refs:
            n = k_ref.shape[0]
            for c0 in range(0, n, key_block):
                c1 = min(c0 + key_block, n)
                s = _dot_nt(qs, k_ref[c0:c1, :])
                bm = jnp.max(s, axis=-1, keepdims=True)
                m_new = bm if m is None else jnp.maximum(m, bm)
                p = jnp.exp((s - m_new).astype(BF16))
                pv = _dot(p, vext[r0 + c0:r0 + c1, :])
                acc = pv if acc is None else acc * jnp.exp(m - m_new) + pv
                m = m_new
            r0 += n
        outs.append(acc[:, :ATTN_DV] / acc[:, ATTN_DV:])
    o = outs[0] - lam_ref[0] * outs[1]
    o = o * lax.rsqrt(jnp.mean(o * o, axis=-1, keepdims=True) + RMS_EPS) * g_ref[...]
    o_ref[...] = (o * (1.0 - lam_init)).astype(o_ref.dtype)


def diff_attention(zq, key_srcs, lam, subln_g, lam_init, batch):
    lq = zq.shape[0] // batch
    tq = min(512, lq)
    nq = lq // tq
    lks = [k.shape[0] // batch for k in key_srcs]
    n_src = len(key_srcs)
    hpb = D_MODEL // LANES
    in_specs = [pl.BlockSpec(memory_space=pltpu.SMEM),
                pl.BlockSpec((1, ATTN_DV), lambda b, h, i: (0, 0)),
                pl.BlockSpec((tq, LANES), lambda b, h, i: (b * nq + i, COL_Q * hpb + h))]
    in_specs += [pl.BlockSpec((lk, LANES), lambda b, h, i: (b, COL_K * hpb + h)) for lk in lks]
    in_specs += [pl.BlockSpec((lk, LANES), lambda b, h, i: (b, COL_V * hpb + h)) for lk in lks]
    return pl.pallas_call(
        functools.partial(_attn_kernel, n_src=n_src, lam_init=lam_init, key_block=1024),
        out_shape=jax.ShapeDtypeStruct((batch * lq, D_MODEL), BF16),
        grid=(batch, ATTN_HEADS, nq),
        in_specs=in_specs,
        out_specs=pl.BlockSpec((tq, LANES), lambda b, h, i: (b * nq + i, h)),
        scratch_shapes=[pltpu.VMEM((sum(lks), 2 * ATTN_DV), BF16)],
        compiler_params=_cparams(("parallel", "parallel", "arbitrary")),
        name="diff_attention",
    )(lam.reshape(1), subln_g.reshape(1, ATTN_DV), zq, *key_srcs, *key_srcs)


def _merge_kernel(y_ref, u_ref, ac_ref, aa_ref, zg0_ref, zg1_ref, zg2_ref, x_ref,
                  d_ref, wglu_ref, ws_ref, wc_ref, wa_ref, wo_ref, g1_ref, sc2_ref, sh2_ref,
                  lng_ref, lnb_ref, wr_hi_ref, wr_lo_ref, x1_ref, h2_ref, lg_ref, *, alpha):
    y = y_ref[...] + d_ref[...] * u_ref[...].astype(F32)
    g = 0.5 * y * (1.0 + jnp.tanh(math.sqrt(2.0 / math.pi) * (y + 0.044715 * (y * y * y))))
    a_ssm = g * _sigmoid(_dot(g.astype(BF16), wglu_ref[...]))
    y_ssm = _dot(a_ssm.astype(BF16), ws_ref[...])
    y_conv = _dot(ac_ref[...], wc_ref[...])
    y_attn = _dot(aa_ref[...], wa_ref[...])
    m = (_sigmoid(zg0_ref[...].astype(F32)) * y_conv + _sigmoid(zg1_ref[...].astype(F32)) * y_ssm
         + _sigmoid(zg2_ref[...].astype(F32)) * y_attn)
    yo = _dot(m.astype(BF16), wo_ref[...])
    x1 = _ln(alpha * x_ref[...] + g1_ref[...] * yo) * lng_ref[...] + lnb_ref[...]
    x1_ref[...] = x1
    h2 = _ln(x1) * (1.0 + sc2_ref[...]) + sh2_ref[...]
    h2_ref[...] = h2.astype(BF16)
    hi, lo = _split_bf16(h2)
    lg_ref[...] = _dot_nt(wr_hi_ref[...], hi) + _dot_nt(wr_hi_ref[...], lo) + _dot_nt(wr_lo_ref[...], hi)


def merge_residual(y_ssm, z, a_conv, a_attn, x2d, mods, wts, seq_len, alpha):
    m = x2d.shape[0]
    batch = m // seq_len
    tm = min(512, seq_len)
    tps = seq_len // tm
    g1, sc2, sh2 = mods
    nb = g1.shape[0]
    mod_map = (lambda i: (i // tps, 0, 0)) if nb > 1 else (lambda i: (0, 0, 0))
    tile = pl.BlockSpec((tm, D_MODEL), lambda i: (i, 0))
    zcol = lambda cidx: pl.BlockSpec((tm, D_MODEL), lambda i: (i, cidx))
    modspec = pl.BlockSpec((None, 1, D_MODEL), mod_map)
    rowspec = pl.BlockSpec((1, D_MODEL), lambda i: (0, 0))
    wspec = pl.BlockSpec((D_MODEL, D_MODEL), lambda i: (0, 0))
    wrspec = pl.BlockSpec((N_EXPERTS, D_MODEL), lambda i: (0, 0))
    return pl.pallas_call(
        functools.partial(_merge_kernel, alpha=alpha),
        out_shape=(jax.ShapeDtypeStruct((m, D_MODEL), F32), jax.ShapeDtypeStruct((m, D_MODEL), BF16),
                   jax.ShapeDtypeStruct((batch, N_EXPERTS, seq_len), F32)),
        grid=(m // tm,),
        in_specs=[tile, zcol(COL_U), tile, tile, zcol(COL_GATE0), zcol(COL_GATE0 + 1), zcol(COL_GATE0 + 2),
                  tile, rowspec, wspec, wspec, wspec, wspec, wspec, modspec, modspec, modspec,
                  rowspec, rowspec, wrspec, wrspec],
        out_specs=(tile, tile, pl.BlockSpec((None, N_EXPERTS, tm), lambda i: (i // tps, 0, i % tps))),
        compiler_params=_cparams(("parallel",)),
        name="merge_residual",
    )(y_ssm, z, a_conv, a_attn, z, z, z, x2d, wts['ssm_d'], wts['w_ssm_glu'], wts['w_ssm_out'],
      wts['w_conv_out'], wts['w_attn_out'], wts['w_o'], g1, sc2, sh2, wts['ln1_g'], wts['ln1_b'],
      wts['wr_hi'], wts['wr_lo'])


def _select_kernel(lg_ref, tri_ref, sel_ref, aff_ref, csum, *, cap):
    lg = lg_ref[...]
    n = lg.shape[1]
    e = jnp.exp(lg - jnp.max(lg, axis=0, keepdims=True))
    aff = e / jnp.sum(e, axis=0, keepdims=True)
    aff_ref[...] = aff
    bits = pltpu.bitcast(aff, jnp.int32)
    capf = float(cap)

    def count(mask):
        return jnp.sum(jnp.where(mask, 1.0, 0.0), axis=1, keepdims=True)

    thr = jnp.zeros((lg.shape[0], 1), jnp.int32)
    for bit in range(30, -1, -1):
        cand = thr | (1 << bit)
        thr = jnp.where(count(bits >= cand) >= capf, cand, thr)
    gt = bits > thr
    eq = bits == thr
    need = capf - count(gt)

    def excl_cumsum(mask):
        x = jnp.where(mask, 1.0, 0.0)
        carry = jnp.zeros((lg.shape[0], 1), F32)
        for j in range(n // LANES):
            blk = x[:, j * LANES:(j + 1) * LANES]
            inc = _dot(blk.astype(BF16), tri_ref[...])
            csum[:, j * LANES:(j + 1) * LANES] = inc - blk + carry
            carry = carry + inc[:, LANES - 1:LANES]
        return csum[...]

    mask = gt | (eq & (excl_cumsum(eq) < need))
    pos = excl_cumsum(mask)
    sel_ref[...] = jnp.where(mask, pos, -1.0)


def route_select(logits_t, cap):
    batch, n_e, n = logits_t.shape
    tri = jnp.asarray(np.triu(np.ones((LANES, LANES), np.float32)), BF16)
    spec = pl.BlockSpec((None, n_e, n), lambda b: (b, 0, 0))
    shp = jax.ShapeDtypeStruct((batch, n_e, n), F32)
    return pl.pallas_call(
        functools.partial(_select_kernel, cap=cap),
        out_shape=(shp, shp),
        grid=(batch,),
        in_specs=[spec, pl.BlockSpec((LANES, LANES), lambda b: (0, 0))],
        out_specs=(spec, spec),
        scratch_shapes=[pltpu.VMEM((n_e, n), F32)],
        compiler_params=_cparams(("parallel",)),
        name="route_select",
    )(logits_t, tri)


def _gather_kernel(sel_ref, h_ref, xs_ref):
    cap = xs_ref.shape[0]
    slot = lax.broadcasted_iota(jnp.int32, (cap, 1), 0).astype(F32)
    onehot = jnp.where(sel_ref[...] == slot, 1.0, 0.0).astype(BF16)
    xs_ref[...] = _dot(onehot, h_ref[...]).astype(xs_ref.dtype)


def gather_tokens(sel4, h2, cap):
    batch, n_e, _, n = sel4.shape
    return pl.pallas_call(
        _gather_kernel,
        out_shape=jax.ShapeDtypeStruct((n_e, batch * cap, D_MODEL), BF16),
        grid=(batch, n_e),
        in_specs=[pl.BlockSpec((None, None, 1, n), lambda b, e: (b, e, 0, 0)),
                  pl.BlockSpec((n, D_MODEL), lambda b, e: (b, 0))],
        out_specs=pl.BlockSpec((None, cap, D_MODEL), lambda b, e: (e, b, 0)),
        compiler_params=_cparams(("parallel", "arbitrary")),
        name="gather_tokens",
    )(sel4, h2)


def _experts_kernel(*refs, n_grp, caps, batch):
    wg_ref, wu_ref, wd_ref = refs[:3]
    xs_refs = refs[3:3 + n_grp]
    sel_refs = refs[3 + n_grp:3 + 2 * n_grp]
    aff_refs = refs[3 + 2 * n_grp:3 + 3 * n_grp]
    ye_refs = refs[3 + 3 * n_grp:3 + 4 * n_grp]
    acc_refs = refs[3 + 4 * n_grp:]
    f = pl.program_id(1)
    nf = pl.num_programs(1)
    @pl.when(f == 0)
    def _():
        for acc_ref in acc_refs:
            acc_ref[...] = jnp.zeros(acc_ref.shape, F32)

    wg = wg_ref[...].astype(BF16)
    wu = wu_ref[...].astype(BF16)
    wd = wd_ref[...].astype(BF16)
    for xs_ref, acc_ref in zip(xs_refs, acc_refs):
        rows = xs_ref.shape[0]
        step = min(512, rows)
        for r0 in range(0, rows, step):
            x = xs_ref[r0:r0 + step, :]
            a = _dot(x, wg)
            act = (a * _sigmoid(a) * _dot(x, wu)).astype(BF16)
            acc_ref[r0:r0 + step, :] += _dot(act, wd)

    @pl.when(f == nf - 1)
    def _():
        for sel_ref, aff_ref, ye_ref, acc_ref, cap in zip(sel_refs, aff_refs, ye_refs, acc_refs, caps):
            slot = lax.broadcasted_iota(jnp.int32, (cap, 1), 0).astype(F32)
            for b in range(batch):
                gate = jnp.sum(jnp.where(sel_ref[b] == slot, aff_ref[b], 0.0), axis=1, keepdims=True)
                rs = slice(b * cap, (b + 1) * cap)
                ye_ref[rs, :] = (acc_ref[rs, :] * gate).astype(ye_ref.dtype)


def expert_ffn(groups, w_gate_up, w_down, layer, batch):
    tf = 256
    nf = D_EXPERT // tf
    n_grp = len(groups)
    caps = tuple(g[3] for g in groups)
    xs_specs = [pl.BlockSpec((None, g[0].shape[1], D_MODEL), lambda e, f: (e, 0, 0)) for g in groups]
    sa_specs = [pl.BlockSpec((batch, None, 1, g[1].shape[3]), lambda e, f: (0, e, 0, 0)) for g in groups]
    return pl.pallas_call(
        functools.partial(_experts_kernel, n_grp=n_grp, caps=caps, batch=batch),
        out_shape=tuple(jax.ShapeDtypeStruct(g[0].shape, BF16) for g in groups),
        grid=(N_EXPERTS, nf),
        in_specs=[pl.BlockSpec((None, None, D_MODEL, tf), lambda e, f: (layer, e, 0, f)),
                  pl.BlockSpec((None, None, D_MODEL, tf), lambda e, f: (layer, e, 0, nf + f)),
                  pl.BlockSpec((None, None, tf, D_MODEL), lambda e, f: (layer, e, f, 0))]
        + xs_specs + sa_specs + sa_specs,
        out_specs=tuple(xs_specs),
        scratch_shapes=[pltpu.VMEM(g[0].shape[1:], F32) for g in groups],
        compiler_params=_cparams(("parallel", "arbitrary")),
        name="expert_ffn",
    )(w_gate_up, w_gate_up, w_down, *[g[0] for g in groups], *[g[1] for g in groups], *[g[2] for g in groups])


def _combine_kernel(selt_ref, ye_ref, x1_ref, g2_ref, lng_ref, lnb_ref, o_ref, *, alpha):
    n_e, cap, _ = ye_ref.shape
    st = selt_ref[...]
    slot = lax.broadcasted_iota(jnp.int32, (1, cap), 1).astype(F32)
    acc = None
    for e in range(n_e):
        onehot = jnp.where(st[:, e:e + 1] == slot, 1.0, 0.0).astype(BF16)
        part = _dot(onehot, ye_ref[e])
        acc = part if acc is None else acc + part
    o_ref[...] = _ln(alpha * x1_ref[...] + g2_ref[...] * acc) * lng_ref[...] + lnb_ref[...]


def combine_residual(sel_t, ye, x1, g2, ln_g, ln_b, cap, alpha):
    batch, n, n_e = sel_t.shape
    tn = min(512, n)
    nt = n // tn
    nb = g2.shape[0]
    mod_map = (lambda b, t: (b, 0, 0)) if nb > 1 else (lambda b, t: (0, 0, 0))
    rowspec = pl.BlockSpec((1, D_MODEL), lambda b, t: (0, 0))
    tile = pl.BlockSpec((tn, D_MODEL), lambda b, t: (b * nt + t, 0))
    return pl.pallas_call(
        functools.partial(_combine_kernel, alpha=alpha),
        out_shape=jax.ShapeDtypeStruct((batch * n, D_MODEL), F32),
        grid=(batch, nt),
        in_specs=[pl.BlockSpec((None, tn, n_e), lambda b, t: (b, t, 0)),
                  pl.BlockSpec((n_e, cap, D_MODEL), lambda b, t: (0, b, 0)),
                  tile, pl.BlockSpec((None, 1, D_MODEL), mod_map), rowspec, rowspec],
        out_specs=tile,
        compiler_params=_cparams(("parallel", "arbitrary")),
        name="combine_residual",
    )(sel_t, ye, x1, g2, ln_g, ln_b)


def kernel(x, c, ctx, c_ctx, w_ada, b_ada, w_in, conv_w, conv_b, conv_ln_g, conv_ln_b, w_conv_out, ssm_a_re, ssm_a_im, ssm_log_dt, ssm_b_re, ssm_b_im, ssm_c_re, ssm_c_im, ssm_d, w_ssm_glu, w_ssm_out, attn_lambda, attn_subln_g, w_attn_out, w_o, ln1_g, ln1_b, w_router, w_gate_up, w_down, ln2_g, ln2_b):
    batch, seq_len, d = x.shape
    ctx_len = ctx.shape[1]
    depth = w_in.shape[0]
    alpha = (2.0 * depth) ** 0.25
    cap = CAPACITY_FACTOR * seq_len // N_EXPERTS
    cap_c = CAPACITY_FACTOR * ctx_len // N_EXPERTS
    tabs = rope_tables(seq_len)
    tabs_c = tuple(t[:ctx_len] for t in tabs)
    row = lambda a: a.reshape(1, d)
    cc = jnp.zeros((8, d), F32).at[:batch].set(c).at[batch].set(c_ctx)

    x2 = x.reshape(batch * seq_len, d)
    xc2 = ctx.reshape(batch * ctx_len, d)
    for l in range(depth):
        ctx_out = l < depth - 1
        lam_init = 0.8 - 0.6 * math.exp(-0.3 * l)
        mod = ada_modulation(cc, w_ada[l], b_ada[l])
        part = lambda rows, k: mod[rows, k * d:(k + 1) * d][:, None, :]
        lat = [part(slice(0, batch), k) for k in range(6)]
        cxm = [part(slice(batch, batch + 1), k) for k in range(6)]
        w_in_b = w_in[l].astype(BF16)
        wr_hi, wr_lo = _split_bf16(w_router[l].T)
        wts = {'ssm_d': row(ssm_d[l]), 'w_ssm_glu': w_ssm_glu[l].astype(BF16), 'w_ssm_out': w_ssm_out[l].astype(BF16),
               'w_conv_out': w_conv_out[l].astype(BF16), 'w_attn_out': w_attn_out[l].astype(BF16),
               'w_o': w_o[l].astype(BF16), 'ln1_g': row(ln1_g[l]), 'ln1_b': row(ln1_b[l]),
               'wr_hi': wr_hi, 'wr_lo': wr_lo}
        lq1, lk1, lq2, lk2 = [attn_lambda[l, i].astype(F32) for i in range(4)]
        lam = jnp.exp(jnp.sum(lq1 * lk1)) - jnp.exp(jnp.sum(lq2 * lk2)) + lam_init
        ssm_ops = ssm_prepare(ssm_a_re[l], ssm_a_im[l], ssm_log_dt[l], ssm_b_re[l], ssm_b_im[l],
                              ssm_c_re[l], ssm_c_im[l])

        z = input_projection(x2, lat[1], lat[0], w_in_b, tabs, seq_len, rope=True)
        zc = input_projection(xc2, cxm[1], cxm[0], w_in_b, tabs_c, ctx_len, rope=False)

        a_conv = conv_branch(z, seq_len, conv_w[l], conv_b[l], conv_ln_g[l], conv_ln_b[l])
        ucols = slice(COL_U * d, (COL_U + 1) * d)
        u_all = jnp.concatenate([zc.reshape(batch, ctx_len, -1)[:, :, ucols],
                                 z.reshape(batch, seq_len, -1)[:, :, ucols]], axis=1)
        y_all = ssm_token_layout(ssm_mixer(ssm_chunk_layout(u_all), ssm_ops, ctx_len // SSM_CHUNK), batch)
        y_ssm_c, y_ssm = y_all[:, :ctx_len].reshape(-1, d), y_all[:, ctx_len:].reshape(-1, d)
        a_attn = diff_attention(z, [zc, z], lam, attn_subln_g[l], lam_init, batch)
        x1, h2, lg = merge_residual(y_ssm, z, a_conv, a_attn, x2, (lat[2], lat[4], lat[3]), wts, seq_len, alpha)
        sel, aff = route_select(lg, cap)
        sel4, aff4 = sel[:, :, None, :], aff[:, :, None, :]
        groups = [(gather_tokens(sel4, h2, cap), sel4, aff4, cap)]
        if ctx_out:
            a_conv_c = conv_branch(zc, ctx_len, conv_w[l], conv_b[l], conv_ln_g[l], conv_ln_b[l])
            a_attn_c = diff_attention(zc, [zc], lam, attn_subln_g[l], lam_init, batch)
            xc1, hc2, lgc = merge_residual(y_ssm_c, zc, a_conv_c, a_attn_c, xc2,
                                           (cxm[2], cxm[4], cxm[3]), wts, ctx_len, alpha)
            selc, affc = route_select(lgc, cap_c)
            selc4, affc4 = selc[:, :, None, :], affc[:, :, None, :]
            groups.append((gather_tokens(selc4, hc2, cap_c), selc4, affc4, cap_c))
        ye = expert_ffn(groups, w_gate_up, w_down, l, batch)
        x2 = combine_residual(jnp.swapaxes(sel, 1, 2), ye[0], x1, lat[5], row(ln2_g[l]), row(ln2_b[l]), cap, alpha)
        if ctx_out:
            xc2 = combine_residual(jnp.swapaxes(selc, 1, 2), ye[1], xc1, cxm[5], row(ln2_g[l]), row(ln2_b[l]),
                                   cap_c, alpha)
    return x2.reshape(batch, seq_len, d)
```

```python
import functools
import math

import jax
import jax.numpy as jnp
import numpy as np
from jax import lax
from jax.experimental import pallas as pl
from jax.experimental.pallas import tpu as pltpu

F32 = jnp.float32
BF16 = jnp.bfloat16

D_MODEL = 1024
GRID_W = 64
N_BRANCH = 3
CONV_WIDTH = 31
CONV_HALO = 16
SSM_GROUP = 16
SSM_GROUPS = D_MODEL // SSM_GROUP
SSM_STATE = 64
SSM_CHUNK = 16
SSM_ROWS = 8
GROUP_W = SSM_CHUNK * SSM_GROUP
SSM_GB = 2
ATTN_HEADS = 8
ATTN_DH = 64
ATTN_DV = 128
ROPE_BASE = 10000.0
N_EXPERTS = 16
D_EXPERT = 2816
CAPACITY_FACTOR = 2
LN_EPS = 1e-6
RMS_EPS = 1e-5
LANES = 128

COL_CONV_V, COL_CONV_G, COL_GATE0, COL_Q, COL_U, COL_K, COL_V = 0, 1, 2, 5, 6, 7, 8
N_COLS = 9

VMEM_LIMIT = 56 * 1024 * 1024


def _cparams(sem, vmem=VMEM_LIMIT):
    return pltpu.CompilerParams(dimension_semantics=sem, vmem_limit_bytes=vmem)


def _ln(x):
    mu = jnp.mean(x, axis=-1, keepdims=True)
    xc = x - mu
    var = jnp.mean(xc * xc, axis=-1, keepdims=True)
    return xc * lax.rsqrt(var + LN_EPS)


def _sigmoid(x):
    return 0.5 * jnp.tanh(0.5 * x) + 0.5


def _dot(a, b):
    return jnp.dot(a, b, preferred_element_type=F32)


def _dot_nt(a, b):
    return lax.dot_general(a, b, (((1,), (1,)), ((), ())), preferred_element_type=F32)


def _split_bf16(x):
    hi = x.astype(BF16)
    lo = (x - hi.astype(F32)).astype(BF16)
    return hi, lo


def _ada_kernel(c_ref, w_ref, b_ref, o_ref):
    c = c_ref[...]
    s = c * _sigmoid(c)
    hi, lo = _split_bf16(s)
    whi, wlo = _split_bf16(w_ref[...])
    o_ref[...] = _dot(hi, whi) + _dot(hi, wlo) + _dot(lo, whi) + b_ref[...]


def ada_modulation(cc, w_ada, b_ada):
    n = w_ada.shape[1]
    tn = 1024
    return pl.pallas_call(
        _ada_kernel,
        out_shape=jax.ShapeDtypeStruct((8, n), F32),
        grid=(n // tn,),
        in_specs=[pl.BlockSpec((8, D_MODEL), lambda j: (0, 0)),
                  pl.BlockSpec((D_MODEL, tn), lambda j: (0, j)),
                  pl.BlockSpec((1, tn), lambda j: (0, j))],
        out_specs=pl.BlockSpec((8, tn), lambda j: (0, j)),
        compiler_params=_cparams(("arbitrary",)),
        name="ada_modulation",
    )(cc, w_ada, b_ada.reshape(1, n))


def _inproj_kernel(x_ref, sc_ref, sh_ref, w_ref, cos_ref, sa_ref, sb_ref, o_ref, h_ref, *, rope):
    j = pl.program_id(1)

    @pl.when(j == 0)
    def _():
        h = _ln(x_ref[...]) * (1.0 + sc_ref[...]) + sh_ref[...]
        h_ref[...] = h.astype(BF16)

    acc = _dot(h_ref[...], w_ref[...])
    tn = acc.shape[1]

    def roped(scale):
        cos, sa, sb = cos_ref[...], sa_ref[...], sb_ref[...]
        for hh in range(tn // LANES):
            seg = acc[:, hh * LANES:(hh + 1) * LANES]
            r = seg * cos + pltpu.roll(seg, LANES - 16, 1) * sa + pltpu.roll(seg, 16, 1) * sb
            o_ref[:, hh * LANES:(hh + 1) * LANES] = (r * scale).astype(o_ref.dtype)

    is_q = j == COL_Q
    is_k = j == COL_K
    if rope:
        @pl.when(is_q)
        def _():
            roped(ATTN_DH ** -0.5)

        @pl.when(is_k)
        def _():
            roped(1.0)

        @pl.when(jnp.logical_not(is_q | is_k))
        def _():
            o_ref[...] = acc.astype(o_ref.dtype)
    else:
        @pl.when(is_q)
        def _():
            o_ref[...] = (acc * (ATTN_DH ** -0.5)).astype(o_ref.dtype)

        @pl.when(jnp.logical_not(is_q))
        def _():
            o_ref[...] = acc.astype(o_ref.dtype)


def input_projection(x2d, sc, sh, w_bf16, rope_tabs, seq_len, rope):
    m = x2d.shape[0]
    tm = min(1024, seq_len)
    tps = seq_len // tm
    nb = sc.shape[0]
    cos, sa, sb = rope_tabs
    mod_map = (lambda i, j: (i // tps, 0, 0)) if nb > 1 else (lambda i, j: (0, 0, 0))
    tab_map = lambda i, j: (i % tps, 0)
    return pl.pallas_call(
        functools.partial(_inproj_kernel, rope=rope),
        out_shape=jax.ShapeDtypeStruct((m, N_COLS * D_MODEL), BF16),
        grid=(m // tm, N_COLS),
        in_specs=[pl.BlockSpec((tm, D_MODEL), lambda i, j: (i, 0)),
                  pl.BlockSpec((None, 1, D_MODEL), mod_map),
                  pl.BlockSpec((None, 1, D_MODEL), mod_map),
                  pl.BlockSpec((D_MODEL, D_MODEL), lambda i, j: (0, j)),
                  pl.BlockSpec((tm, LANES), tab_map),
                  pl.BlockSpec((tm, LANES), tab_map),
                  pl.BlockSpec((tm, LANES), tab_map)],
        out_specs=pl.BlockSpec((tm, D_MODEL), lambda i, j: (i, j)),
        scratch_shapes=[pltpu.VMEM((tm, D_MODEL), BF16)],
        compiler_params=_cparams(("parallel", "arbitrary")),
        name="input_projection",
    )(x2d, sc, sh, w_bf16, cos, sa, sb)


def rope_tables(length):
    rows = length // GRID_W
    row = jnp.repeat(jnp.arange(rows), GRID_W)
    col = jnp.tile(jnp.arange(GRID_W), rows)
    n_freq = ATTN_DH // 4
    inv_freq = ROPE_BASE ** (-jnp.arange(n_freq, dtype=F32) / n_freq)
    ang = jnp.stack([row, col], -1).astype(F32)[:, :, None, None] * inv_freq
    ang = jnp.broadcast_to(ang, (length, 2, 2, n_freq)).reshape(length, ATTN_DH)
    cos, sin = jnp.cos(ang), jnp.sin(ang)
    first_half = (jnp.arange(ATTN_DH) % 32) < 16
    sa = jnp.where(first_half, -sin, 0.0)
    sb = jnp.where(first_half, 0.0, sin)
    tile2 = lambda t: jnp.concatenate([t, t], axis=-1)
    return tile2(cos), tile2(sa), tile2(sb)


def _conv_kernel(cur_ref, prev_ref, next_ref, w_ref, b_ref, g_ref, beta_ref, o_ref, gbuf, ybuf, *, tps):
    ti = pl.program_id(0) % tps
    tm = cur_ref.shape[0]
    d = D_MODEL

    def glu(ref):
        return ref[:, :d].astype(F32) * _sigmoid(ref[:, d:].astype(F32))

    gbuf[CONV_HALO:CONV_HALO + tm, :] = glu(cur_ref)
    gbuf[0:CONV_HALO, :] = jnp.where(ti > 0, glu(prev_ref), 0.0)
    gbuf[CONV_HALO + tm:, :] = jnp.where(ti < tps - 1, glu(next_ref), 0.0)
    off = CONV_HALO - CONV_WIDTH // 2
    rows = min(tm, 128)
    for c in range(d // LANES):
        cs = slice(c * LANES, (c + 1) * LANES)
        for r0 in range(0, tm, rows):
            acc = None
            for res in range(8):
                part = None
                for o in range(res, off + CONV_WIDTH, 8):
                    if o < off:
                        continue
                    term = gbuf[r0 + o - res:r0 + o - res + rows + 8, cs] * w_ref[o - off:o - off + 1, cs]
                    part = term if part is None else part + term
                part = part[res:res + rows, :]
                acc = part if acc is None else acc + part
            ybuf[r0:r0 + rows, cs] = acc
    y = _ln(ybuf[...] + b_ref[...]) * g_ref[...] + beta_ref[...]
    o_ref[...] = (y * _sigmoid(y)).astype(o_ref.dtype)


def conv_branch(z, seq_len, conv_w, conv_b, ln_g, ln_b):
    m = z.shape[0]
    tm = min(256, seq_len)
    tps = seq_len // tm
    hb = tm // CONV_HALO
    nblk = m // CONV_HALO
    row = lambda a: a.reshape(1, D_MODEL)
    return pl.pallas_call(
        functools.partial(_conv_kernel, tps=tps),
        out_shape=jax.ShapeDtypeStruct((m, D_MODEL), BF16),
        grid=(m // tm,),
        in_specs=[pl.BlockSpec((tm, 2 * D_MODEL), lambda i: (i, 0)),
                  pl.BlockSpec((CONV_HALO, 2 * D_MODEL), lambda i: (jnp.maximum(i * hb - 1, 0), 0)),
                  pl.BlockSpec((CONV_HALO, 2 * D_MODEL), lambda i: (jnp.minimum((i + 1) * hb, nblk - 1), 0)),
                  pl.BlockSpec((CONV_WIDTH, D_MODEL), lambda i: (0, 0)),
                  pl.BlockSpec((1, D_MODEL), lambda i: (0, 0)),
                  pl.BlockSpec((1, D_MODEL), lambda i: (0, 0)),
                  pl.BlockSpec((1, D_MODEL), lambda i: (0, 0))],
        out_specs=pl.BlockSpec((tm, D_MODEL), lambda i: (i, 0)),
        scratch_shapes=[pltpu.VMEM((tm + 2 * CONV_HALO, D_MODEL), F32), pltpu.VMEM((tm, D_MODEL), F32)],
        compiler_params=_cparams(("parallel",)),
        name="conv_branch",
    )(z, z, z, conv_w, row(conv_b), row(ln_g), row(ln_b))


def ssm_prepare(a_re, a_im, log_dt, b_re, b_im, c_re, c_im):
    a = lax.complex(a_re.astype(F32), a_im.astype(F32))
    dt_a = jnp.exp(log_dt.astype(F32))[..., None] * a
    b = lax.complex(b_re.astype(F32), b_im.astype(F32))
    b_bar = ((jnp.exp(dt_a) - 1.0) / a)[..., None] * b
    cc = lax.complex(c_re.astype(F32), c_im.astype(F32))
    n = SSM_CHUNK
    steps = jnp.arange(n + 1, dtype=F32)
    pw = jnp.exp(steps[None, None, :, None] * dt_a[:, :, None, :])
    lag_resp = jnp.real(jnp.einsum('dgop,dgkp,dgpi->dgkio', cc, pw[:, :, :n], b_bar))
    idx = jnp.arange(n)

    def toeplitz(resp, lag):
        t = resp[:, jnp.clip(lag, 0, n - 1)]
        t = jnp.where((lag >= 0)[None, :, :, None, None], t, 0.0)
        return t.transpose(0, 1, 3, 2, 4).reshape(SSM_GROUPS, GROUP_W, GROUP_W)

    t_sum = toeplitz(lag_resp[0], idx[None, :] - idx[:, None]) + toeplitz(lag_resp[1], idx[:, None] - idx[None, :])

    def parts(z, sign):
        return [jnp.real(z), sign * jnp.imag(z)]

    w_f = jnp.einsum('gsp,gpi->gsip', pw[0][:, n - 1 - idx], b_bar[0]).reshape(SSM_GROUPS, GROUP_W, SSM_STATE)
    w_r = jnp.einsum('gsp,gpi->gsip', pw[1][:, idx], b_bar[1]).reshape(SSM_GROUPS, GROUP_W, SSM_STATE)
    w_cat = jnp.concatenate(parts(w_f, 1.0) + parts(w_r, 1.0), axis=-1)
    v_f = jnp.einsum('gop,gtp->gpto', cc[0], pw[0][:, 1 + idx]).reshape(SSM_GROUPS, SSM_STATE, GROUP_W)
    v_r = jnp.einsum('gop,gtp->gpto', cc[1], pw[1][:, n - idx]).reshape(SSM_GROUPS, SSM_STATE, GROUP_W)
    v_cat = jnp.concatenate(parts(v_f, -1.0) + parts(v_r, -1.0), axis=1)
    a_n = pw[:, :, n]
    mul_same = jnp.concatenate([jnp.real(a_n[0])] * 2 + [jnp.real(a_n[1])] * 2, axis=-1)
    mul_swap = jnp.concatenate([-jnp.imag(a_n[0]), jnp.imag(a_n[0]), -jnp.imag(a_n[1]), jnp.imag(a_n[1])],
                               axis=-1)
    flat = lambda t: t.reshape(1, SSM_GROUPS * GROUP_W)
    return _split_bf16(t_sum) + _split_bf16(w_cat) + _split_bf16(v_cat) + (flat(mul_same), flat(mul_swap))


def _chunk_perm(octet_lanes):
    p = np.zeros((octet_lanes, octet_lanes), np.float32)
    for s in range(SSM_CHUNK):
        for gl in range(LANES // SSM_GROUP):
            for i in range(SSM_GROUP):
                p[s * LANES + gl * SSM_GROUP + i, gl * GROUP_W + s * SSM_GROUP + i] = 1.0
    return p


def _ssm_in_kernel(*refs):
    xs_refs, (perm_ref, u2_ref) = refs[:SSM_CHUNK], refs[SSM_CHUNK:]
    lhs = jnp.concatenate([x[...] for x in xs_refs], axis=1)
    u2_ref[...] = _dot(lhs, perm_ref[...]).astype(u2_ref.dtype)


def ssm_chunk_layout(z, seq_len):
    rows = z.shape[0] // SSM_CHUNK
    z16 = z.reshape(rows, SSM_CHUNK * N_COLS * D_MODEL)
    rb = min(256, rows)
    octets = D_MODEL // LANES
    ow = octets * GROUP_W
    lane_blocks = N_COLS * D_MODEL // LANES
    perm = jnp.asarray(_chunk_perm(ow), BF16)
    piece = lambda s: pl.BlockSpec((rb, LANES), lambda r, j: (r, s * lane_blocks + COL_U * octets + j))
    return pl.pallas_call(
        _ssm_in_kernel,
        out_shape=jax.ShapeDtypeStruct((rows, SSM_GROUPS * GROUP_W), BF16),
        grid=(rows // rb, octets),
        in_specs=[piece(s) for s in range(SSM_CHUNK)] + [pl.BlockSpec((ow, ow), lambda r, j: (0, 0))],
        out_specs=pl.BlockSpec((rb, ow), lambda r, j: (r, j)),
        compiler_params=_cparams(("parallel", "arbitrary")),
        name="ssm_chunk_layout",
    )(*([z16] * SSM_CHUNK), perm)


def _ssm_out_kernel(y_ref, perm_ref, o_ref):
    y_hi, y_lo = _split_bf16(y_ref[...])
    o_ref[...] = _dot(y_hi, perm_ref[...]) + _dot(y_lo, perm_ref[...])


def ssm_token_octets(y2):
    rows = y2.shape[0]
    rb = min(256, rows)
    octets = D_MODEL // LANES
    ow = octets * GROUP_W
    perm_t = jnp.asarray(_chunk_perm(ow).T, BF16)
    blk = pl.BlockSpec((rb, ow), lambda r, j: (r, j))
    return pl.pallas_call(
        _ssm_out_kernel,
        out_shape=jax.ShapeDtypeStruct(y2.shape, F32),
        grid=(rows // rb, octets),
        in_specs=[blk, pl.BlockSpec((ow, ow), lambda r, j: (0, 0))],
        out_specs=blk,
        compiler_params=_cparams(("parallel", "arbitrary")),
        name="ssm_token_octets",
    )(y2, perm_t)


def _ssm_kernel(uc_ref, ul_ref, t_hi_ref, t_lo_ref, w_hi_ref, w_lo_ref, v_hi_ref, v_lo_ref, same_ref, swap_ref,
                yc_ref, yl_ref, sbuf, xbuf, *, batch):
    ncol, srows, _ = sbuf.shape
    nch = srows // SSM_ROWS
    n_ctx = uc_ref.shape[0] // batch
    n_lat = ul_ref.shape[0] // batch
    groups = t_hi_ref.shape[0]
    gsl = lambda k: slice(k * GROUP_W, (k + 1) * GROUP_W)

    def chunk_rows(b, first, count):
        return pl.ds(first * SSM_ROWS + b, count, stride=SSM_ROWS)

    sbuf[...] = jnp.zeros(sbuf.shape, F32)
    for u_ref, first, count in ((uc_ref, 0, n_ctx), (ul_ref, n_ctx, n_lat)):
        for k in range(groups):
            u = u_ref[:, gsl(k)]
            inc = _dot(u, w_hi_ref[k]) + _dot(u, w_lo_ref[k])
            for v in range(2):
                for b in range(batch):
                    sbuf[2 * k + v, chunk_rows(b, first, count), :] = inc[b * count:(b + 1) * count,
                                                                          v * LANES:(v + 1) * LANES]
    for v in range(ncol):
        xbuf[v] = pltpu.roll(sbuf[v], LANES // 2, 1)

    same = [jnp.broadcast_to(same_ref[:, v * LANES:(v + 1) * LANES], (SSM_ROWS, LANES)) for v in range(ncol)]
    swap = [jnp.broadcast_to(swap_ref[:, v * LANES:(v + 1) * LANES], (SSM_ROWS, LANES)) for v in range(ncol)]

    def body(j, state):
        cr = jnp.where(j < n_ctx, n_ctx - 1 - j, nch - 1 - (j - n_ctx))
        new = []
        for v in range(ncol):
            ci = j if v % 2 == 0 else cr
            rs = pl.ds(pl.multiple_of(ci * SSM_ROWS, SSM_ROWS), SSM_ROWS)
            s, sx = state[2 * v], state[2 * v + 1]
            inc = sbuf[v, rs, :]
            sbuf[v, rs, :] = s
            new.append(s * same[v] + sx * swap[v] + inc)
            new.append(sx * same[v] - s * swap[v] + xbuf[v, rs, :])
        return tuple(new)

    lax.fori_loop(0, nch, body, tuple(jnp.zeros((SSM_ROWS, LANES), F32) for _ in range(2 * ncol)), unroll=4)

    for u_ref, y_ref, first, count in ((uc_ref, yc_ref, 0, n_ctx), (ul_ref, yl_ref, n_ctx, n_lat)):
        for k in range(groups):
            u = u_ref[:, gsl(k)]
            st = jnp.concatenate(
                [jnp.concatenate([sbuf[2 * k + v, chunk_rows(b, first, count), :] for b in range(batch)], axis=0)
                 for v in range(2)], axis=1)
            s_hi, s_lo = _split_bf16(st)
            y_ref[:, gsl(k)] = (_dot(u, t_hi_ref[k]) + _dot(u, t_lo_ref[k]) + _dot(s_hi, v_hi_ref[k])
                                + _dot(s_hi, v_lo_ref[k]) + _dot(s_lo, v_hi_ref[k]))


def ssm_mixer(u2c, u2l, ops, batch):
    t_hi, t_lo, w_hi, w_lo, v_hi, v_lo, mul_same, mul_swap = ops
    gw = SSM_GB * GROUP_W
    nch = (u2c.shape[0] + u2l.shape[0]) // batch
    col = lambda a: pl.BlockSpec((a.shape[0], gw), lambda j: (0, j))
    wsp = pl.BlockSpec((SSM_GB, GROUP_W, GROUP_W), lambda j: (j, 0, 0))
    tab = pl.BlockSpec((1, gw), lambda j: (0, j))
    return pl.pallas_call(
        functools.partial(_ssm_kernel, batch=batch),
        out_shape=(jax.ShapeDtypeStruct(u2c.shape, F32), jax.ShapeDtypeStruct(u2l.shape, F32)),
        grid=(SSM_GROUPS // SSM_GB,),
        in_specs=[col(u2c), col(u2l)] + [wsp] * 6 + [tab, tab],
        out_specs=(col(u2c), col(u2l)),
        scratch_shapes=[pltpu.VMEM((gw // LANES, nch * SSM_ROWS, LANES), F32)] * 2,
        compiler_params=_cparams(("parallel",)),
        name="ssm_mixer",
    )(u2c, u2l, t_hi, t_lo, w_hi, w_lo, v_hi, v_lo, mul_same, mul_swap)


def _attn_kernel(*refs, n_src, lam_init, key_block):
    lam_ref, g_ref, q_ref = refs[:3]
    k_refs = refs[3:3 + n_src]
    v_refs = refs[3 + n_src:3 + 2 * n_src]
    o_ref, vext = refs[3 + 2 * n_src], refs[4 + 2 * n_src]
    qi = pl.program_id(2)

    @pl.when(qi == 0)
    def _():
        r0 = 0
        for v_ref in v_refs:
            n = v_ref.shape[0]
            vext[r0:r0 + n, :ATTN_DV] = v_ref[...]
            vext[r0:r0 + n, ATTN_DV:] = jnp.ones((n, ATTN_DV), BF16)
            r0 += n

    q = q_ref[...]
    lane = lax.broadcasted_iota(jnp.int32, q.shape, 1)
    outs = []
    for sub in range(2):
        qs = jnp.where((lane >= ATTN_DH) if sub else (lane < ATTN_DH), q, jnp.zeros_like(q))
        m = acc = None
        r0 = 0
        for k_ref in k_refs:
            n = k_ref.shape[0]
            for c0 in range(0, n, key_block):
                c1 = min(c0 + key_block, n)
                s = _dot_nt(qs, k_ref[c0:c1, :])
                bm = jnp.max(s, axis=-1, keepdims=True)
                m_new = bm if m is None else jnp.maximum(m, bm)
                p = jnp.exp((s - m_new).astype(BF16))
                pv = _dot(p, vext[r0 + c0:r0 + c1, :])
                acc = pv if acc is None else acc * jnp.exp(m - m_new) + pv
                m = m_new
            r0 += n
        outs.append(acc[:, :ATTN_DV] / acc[:, ATTN_DV:])
    o = outs[0] - lam_ref[0] * outs[1]
    o = o * lax.rsqrt(jnp.mean(o * o, axis=-1, keepdims=True) + RMS_EPS) * g_ref[...]
    o_ref[...] = (o * (1.0 - lam_init)).astype(o_ref.dtype)


def diff_attention(zq, key_srcs, lam, subln_g, lam_init, batch):
    lq = zq.shape[0] // batch
    tq = min(512, lq)
    nq = lq // tq
    lks = [k.shape[0] // batch for k in key_srcs]
    n_src = len(key_srcs)
    hpb = D_MODEL // LANES
    in_specs = [pl.BlockSpec(memory_space=pltpu.SMEM),
                pl.BlockSpec((1, ATTN_DV), lambda b, h, i: (0, 0)),
                pl.BlockSpec((tq, LANES), lambda b, h, i: (b * nq + i, COL_Q * hpb + h))]
    in_specs += [pl.BlockSpec((lk, LANES), lambda b, h, i: (b, COL_K * hpb + h)) for lk in lks]
    in_specs += [pl.BlockSpec((lk, LANES), lambda b, h, i: (b, COL_V * hpb + h)) for lk in lks]
    return pl.pallas_call(
        functools.partial(_attn_kernel, n_src=n_src, lam_init=lam_init, key_block=1024),
        out_shape=jax.ShapeDtypeStruct((batch * lq, D_MODEL), BF16),
        grid=(batch, ATTN_HEADS, nq),
        in_specs=in_specs,
        out_specs=pl.BlockSpec((tq, LANES), lambda b, h, i: (b * nq + i, h)),
        scratch_shapes=[pltpu.VMEM((sum(lks), 2 * ATTN_DV), BF16)],
        compiler_params=_cparams(("parallel", "parallel", "arbitrary")),
        name="diff_attention",
    )(lam.reshape(1), subln_g.reshape(1, ATTN_DV), zq, *key_srcs, *key_srcs)


def _merge_kernel(y_ref, u_ref, ac_ref, aa_ref, zg0_ref, zg1_ref, zg2_ref, x_ref,
                  d_ref, wglu_ref, ws_ref, wc_ref, wa_ref, wo_ref, g1_ref, sc2_ref, sh2_ref,
                  lng_ref, lnb_ref, wr_hi_ref, wr_lo_ref, x1_ref, h2_ref, lg_ref, ybuf, *, alpha):
    chunks = y_ref.shape[0]
    for j in range(D_MODEL // LANES):
        for t in range(SSM_CHUNK):
            p0 = (j * SSM_CHUNK + t) * LANES
            ybuf[j, pl.ds(t, chunks, stride=SSM_CHUNK), :] = y_ref[:, p0:p0 + LANES]
    y_ssm_raw = jnp.concatenate([ybuf[j] for j in range(D_MODEL // LANES)], axis=1)
    y = y_ssm_raw + d_ref[...] * u_ref[...].astype(F32)
    g = 0.5 * y * (1.0 + jnp.tanh(math.sqrt(2.0 / math.pi) * (y + 0.044715 * (y * y * y))))
    a_ssm = g * _sigmoid(_dot(g.astype(BF16), wglu_ref[...]))
    y_ssm = _dot(a_ssm.astype(BF16), ws_ref[...])
    y_conv = _dot(ac_ref[...], wc_ref[...])
    y_attn = _dot(aa_ref[...], wa_ref[...])
    m = (_sigmoid(zg0_ref[...].astype(F32)) * y_conv + _sigmoid(zg1_ref[...].astype(F32)) * y_ssm
         + _sigmoid(zg2_ref[...].astype(F32)) * y_attn)
    yo = _dot(m.astype(BF16), wo_ref[...])
    x1 = _ln(alpha * x_ref[...] + g1_ref[...] * yo) * lng_ref[...] + lnb_ref[...]
    x1_ref[...] = x1
    h2 = _ln(x1) * (1.0 + sc2_ref[...]) + sh2_ref[...]
    h2_ref[...] = h2.astype(BF16)
    hi, lo = _split_bf16(h2)
    lg_ref[...] = _dot_nt(wr_hi_ref[...], hi) + _dot_nt(wr_hi_ref[...], lo) + _dot_nt(wr_lo_ref[...], hi)


def merge_residual(y_ssm, z, a_conv, a_attn, x2d, mods, wts, seq_len, alpha):
    m = x2d.shape[0]
    batch = m // seq_len
    tm = min(512, seq_len)
    tps = seq_len // tm
    g1, sc2, sh2 = mods
    nb = g1.shape[0]
    mod_map = (lambda i: (i // tps, 0, 0)) if nb > 1 else (lambda i: (0, 0, 0))
    tile = pl.BlockSpec((tm, D_MODEL), lambda i: (i, 0))
    zcol = lambda cidx: pl.BlockSpec((tm, D_MODEL), lambda i: (i, cidx))
    modspec = pl.BlockSpec((None, 1, D_MODEL), mod_map)
    rowspec = pl.BlockSpec((1, D_MODEL), lambda i: (0, 0))
    wspec = pl.BlockSpec((D_MODEL, D_MODEL), lambda i: (0, 0))
    wrspec = pl.BlockSpec((N_EXPERTS, D_MODEL), lambda i: (0, 0))
    return pl.pallas_call(
        functools.partial(_merge_kernel, alpha=alpha),
        out_shape=(jax.ShapeDtypeStruct((m, D_MODEL), F32), jax.ShapeDtypeStruct((m, D_MODEL), BF16),
                   jax.ShapeDtypeStruct((batch, N_EXPERTS, seq_len), F32)),
        grid=(m // tm,),
        in_specs=[pl.BlockSpec((tm // SSM_CHUNK, SSM_GROUPS * GROUP_W), lambda i: (i, 0)),
                  zcol(COL_U), tile, tile, zcol(COL_GATE0), zcol(COL_GATE0 + 1), zcol(COL_GATE0 + 2),
                  tile, rowspec, wspec, wspec, wspec, wspec, wspec, modspec, modspec, modspec,
                  rowspec, rowspec, wrspec, wrspec],
        out_specs=(tile, tile, pl.BlockSpec((None, N_EXPERTS, tm), lambda i: (i // tps, 0, i % tps))),
        scratch_shapes=[pltpu.VMEM((D_MODEL // LANES, tm, LANES), F32)],
        compiler_params=_cparams(("parallel",)),
        name="merge_residual",
    )(y_ssm, z, a_conv, a_attn, z, z, z, x2d, wts['ssm_d'], wts['w_ssm_glu'], wts['w_ssm_out'],
      wts['w_conv_out'], wts['w_attn_out'], wts['w_o'], g1, sc2, sh2, wts['ln1_g'], wts['ln1_b'],
      wts['wr_hi'], wts['wr_lo'])


def _select_kernel(lg_ref, tri_ref, sel_ref, aff_ref, csum, *, cap):
    lg = lg_ref[...]
    n = lg.shape[1]
    e = jnp.exp(lg - jnp.max(lg, axis=0, keepdims=True))
    aff = e / jnp.sum(e, axis=0, keepdims=True)
    aff_ref[...] = aff
    bits = pltpu.bitcast(aff, jnp.int32)
    capf = float(cap)

    def count(mask):
        return jnp.sum(jnp.where(mask, 1.0, 0.0), axis=1, keepdims=True)

    thr = jnp.zeros((lg.shape[0], 1), jnp.int32)
    for bit in range(30, -1, -1):
        cand = thr | (1 << bit)
        thr = jnp.where(count(bits >= cand) >= capf, cand, thr)
    gt = bits > thr
    eq = bits == thr
    need = capf - count(gt)

    def excl_cumsum(mask):
        x = jnp.where(mask, 1.0, 0.0)
        carry = jnp.zeros((lg.shape[0], 1), F32)
        for j in range(n // LANES):
            blk = x[:, j * LANES:(j + 1) * LANES]
            inc = _dot(blk.astype(BF16), tri_ref[...])
            csum[:, j * LANES:(j + 1) * LANES] = inc - blk + carry
            carry = carry + inc[:, LANES - 1:LANES]
        return csum[...]

    mask = gt | (eq & (excl_cumsum(eq) < need))
    pos = excl_cumsum(mask)
    sel_ref[...] = jnp.where(mask, pos, -1.0)


def route_select(logits_t, cap):
    batch, n_e, n = logits_t.shape
    tri = jnp.asarray(np.triu(np.ones((LANES, LANES), np.float32)), BF16)
    spec = pl.BlockSpec((None, n_e, n), lambda b: (b, 0, 0))
    shp = jax.ShapeDtypeStruct((batch, n_e, n), F32)
    return pl.pallas_call(
        functools.partial(_select_kernel, cap=cap),
        out_shape=(shp, shp),
        grid=(batch,),
        in_specs=[spec, pl.BlockSpec((LANES, LANES), lambda b: (0, 0))],
        out_specs=(spec, spec),
        scratch_shapes=[pltpu.VMEM((n_e, n), F32)],
        compiler_params=_cparams(("parallel",)),
        name="route_select",
    )(logits_t, tri)


def _gather_kernel(sel_ref, h_ref, xs_ref):
    cap = xs_ref.shape[0]
    slot = lax.broadcasted_iota(jnp.int32, (cap, 1), 0).astype(F32)
    onehot = jnp.where(sel_ref[...] == slot, 1.0, 0.0).astype(BF16)
    xs_ref[...] = _dot(onehot, h_ref[...]).astype(xs_ref.dtype)


def gather_tokens(sel4, h2, cap):
    batch, n_e, _, n = sel4.shape
    return pl.pallas_call(
        _gather_kernel,
        out_shape=jax.ShapeDtypeStruct((n_e, batch * cap, D_MODEL), BF16),
        grid=(batch, n_e),
        in_specs=[pl.BlockSpec((None, None, 1, n), lambda b, e: (b, e, 0, 0)),
                  pl.BlockSpec((n, D_MODEL), lambda b, e: (b, 0))],
        out_specs=pl.BlockSpec((None, cap, D_MODEL), lambda b, e: (e, b, 0)),
        compiler_params=_cparams(("parallel", "arbitrary")),
        name="gather_tokens",
    )(sel4, h2)


def _experts_kernel(*refs, n_grp, caps, batch):
    wg_ref, wu_ref, wd_ref = refs[:3]
    xs_refs = refs[3:3 + n_grp]
    sel_refs = refs[3 + n_grp:3 + 2 * n_grp]
    aff_refs = refs[3 + 2 * n_grp:3 + 3 * n_grp]
    ye_refs = refs[3 + 3 * n_grp:3 + 4 * n_grp]
    acc_refs = refs[3 + 4 * n_grp:]
    f = pl.program_id(1)
    nf = pl.num_programs(1)
    @pl.when(f == 0)
    def _():
        for acc_ref in acc_refs:
            acc_ref[...] = jnp.zeros(acc_ref.shape, F32)

    wg = wg_ref[...].astype(BF16)
    wu = wu_ref[...].astype(BF16)
    wd = wd_ref[...].astype(BF16)
    for xs_ref, acc_ref in zip(xs_refs, acc_refs):
        rows = xs_ref.shape[0]
        step = min(512, rows)
        for r0 in range(0, rows, step):
            x = xs_ref[r0:r0 + step, :]
            a = _dot(x, wg)
            act = (a * _sigmoid(a) * _dot(x, wu)).astype(BF16)
            acc_ref[r0:r0 + step, :] += _dot(act, wd)

    @pl.when(f == nf - 1)
    def _():
        for sel_ref, aff_ref, ye_ref, acc_ref, cap in zip(sel_refs, aff_refs, ye_refs, acc_refs, caps):
            slot = lax.broadcasted_iota(jnp.int32, (cap, 1), 0).astype(F32)
            for b in range(batch):
                gate = jnp.sum(jnp.where(sel_ref[b] == slot, aff_ref[b], 0.0), axis=1, keepdims=True)
                rs = slice(b * cap, (b + 1) * cap)
                ye_ref[rs, :] = (acc_ref[rs, :] * gate).astype(ye_ref.dtype)


def expert_ffn(groups, w_gate_up, w_down, layer, batch):
    tf = 256
    nf = D_EXPERT // tf
    n_grp = len(groups)
    caps = tuple(g[3] for g in groups)
    xs_specs = [pl.BlockSpec((None, g[0].shape[1], D_MODEL), lambda e, f: (e, 0, 0)) for g in groups]
    sa_specs = [pl.BlockSpec((batch, None, 1, g[1].shape[3]), lambda e, f: (0, e, 0, 0)) for g in groups]
    return pl.pallas_call(
        functools.partial(_experts_kernel, n_grp=n_grp, caps=caps, batch=batch),
        out_shape=tuple(jax.ShapeDtypeStruct(g[0].shape, BF16) for g in groups),
        grid=(N_EXPERTS, nf),
        in_specs=[pl.BlockSpec((None, None, D_MODEL, tf), lambda e, f: (layer, e, 0, f)),
                  pl.BlockSpec((None, None, D_MODEL, tf), lambda e, f: (layer, e, 0, nf + f)),
                  pl.BlockSpec((None, None, tf, D_MODEL), lambda e, f: (layer, e, f, 0))]
        + xs_specs + sa_specs + sa_specs,
        out_specs=tuple(xs_specs),
        scratch_shapes=[pltpu.VMEM(g[0].shape[1:], F32) for g in groups],
        compiler_params=_cparams(("parallel", "arbitrary")),
        name="expert_ffn",
    )(w_gate_up, w_gate_up, w_down, *[g[0] for g in groups], *[g[1] for g in groups], *[g[2] for g in groups])


def _combine_kernel(selt_ref, ye_ref, x1_ref, g2_ref, lng_ref, lnb_ref, o_ref, *, alpha):
    n_e, cap, _ = ye_ref.shape
    st = selt_ref[...]
    slot = lax.broadcasted_iota(jnp.int32, (1, cap), 1).astype(F32)
    acc = None
    for e in range(n_e):
        onehot = jnp.where(st[:, e:e + 1] == slot, 1.0, 0.0).astype(BF16)
        part = _dot(onehot, ye_ref[e])
        acc = part if acc is None else acc + part
    o_ref[...] = _ln(alpha * x1_ref[...] + g2_ref[...] * acc) * lng_ref[...] + lnb_ref[...]


def combine_residual(sel_t, ye, x1, g2, ln_g, ln_b, cap, alpha):
    batch, n, n_e = sel_t.shape
    tn = min(512, n)
    nt = n // tn
    nb = g2.shape[0]
    mod_map = (lambda b, t: (b, 0, 0)) if nb > 1 else (lambda b, t: (0, 0, 0))
    rowspec = pl.BlockSpec((1, D_MODEL), lambda b, t: (0, 0))
    tile = pl.BlockSpec((tn, D_MODEL), lambda b, t: (b * nt + t, 0))
    return pl.pallas_call(
        functools.partial(_combine_kernel, alpha=alpha),
        out_shape=jax.ShapeDtypeStruct((batch * n, D_MODEL), F32),
        grid=(batch, nt),
        in_specs=[pl.BlockSpec((None, tn, n_e), lambda b, t: (b, t, 0)),
                  pl.BlockSpec((n_e, cap, D_MODEL), lambda b, t: (0, b, 0)),
                  tile, pl.BlockSpec((None, 1, D_MODEL), mod_map), rowspec, rowspec],
        out_specs=tile,
        compiler_params=_cparams(("parallel", "arbitrary")),
        name="combine_residual",
    )(sel_t, ye, x1, g2, ln_g, ln_b)


def kernel(x, c, ctx, c_ctx, w_ada, b_ada, w_in, conv_w, conv_b, conv_ln_g, conv_ln_b, w_conv_out, ssm_a_re, ssm_a_im, ssm_log_dt, ssm_b_re, ssm_b_im, ssm_c_re, ssm_c_im, ssm_d, w_ssm_glu, w_ssm_out, attn_lambda, attn_subln_g, w_attn_out, w_o, ln1_g, ln1_b, w_router, w_gate_up, w_down, ln2_g, ln2_b):
    batch, seq_len, d = x.shape
    ctx_len = ctx.shape[1]
    depth = w_in.shape[0]
    alpha = (2.0 * depth) ** 0.25
    cap = CAPACITY_FACTOR * seq_len // N_EXPERTS
    cap_c = CAPACITY_FACTOR * ctx_len // N_EXPERTS
    tabs = rope_tables(seq_len)
    tabs_c = tuple(t[:ctx_len] for t in tabs)
    row = lambda a: a.reshape(1, d)
    cc = jnp.zeros((8, d), F32).at[:batch].set(c).at[batch].set(c_ctx)

    x2 = x.reshape(batch * seq_len, d)
    xc2 = ctx.reshape(batch * ctx_len, d)
    for l in range(depth):
        ctx_out = l < depth - 1
        lam_init = 0.8 - 0.6 * math.exp(-0.3 * l)
        mod = ada_modulation(cc, w_ada[l], b_ada[l])
        part = lambda rows, k: mod[rows, k * d:(k + 1) * d][:, None, :]
        lat = [part(slice(0, batch), k) for k in range(6)]
        cxm = [part(slice(batch, batch + 1), k) for k in range(6)]
        w_in_b = w_in[l].astype(BF16)
        wr_hi, wr_lo = _split_bf16(w_router[l].T)
        wts = {'ssm_d': row(ssm_d[l]), 'w_ssm_glu': w_ssm_glu[l].astype(BF16), 'w_ssm_out': w_ssm_out[l].astype(BF16),
               'w_conv_out': w_conv_out[l].astype(BF16), 'w_attn_out': w_attn_out[l].astype(BF16),
               'w_o': w_o[l].astype(BF16), 'ln1_g': row(ln1_g[l]), 'ln1_b': row(ln1_b[l]),
               'wr_hi': wr_hi, 'wr_lo': wr_lo}
        lq1, lk1, lq2, lk2 = [attn_lambda[l, i].astype(F32) for i in range(4)]
        lam = jnp.exp(jnp.sum(lq1 * lk1)) - jnp.exp(jnp.sum(lq2 * lk2)) + lam_init
        ssm_ops = ssm_prepare(ssm_a_re[l], ssm_a_im[l], ssm_log_dt[l], ssm_b_re[l], ssm_b_im[l],
                              ssm_c_re[l], ssm_c_im[l])

        z = input_projection(x2, lat[1], lat[0], w_in_b, tabs, seq_len, rope=True)
        zc = input_projection(xc2, cxm[1], cxm[0], w_in_b, tabs_c, ctx_len, rope=False)

        a_conv = conv_branch(z, seq_len, conv_w[l], conv_b[l], conv_ln_g[l], conv_ln_b[l])
        y_chunks_c, y_chunks = ssm_mixer(ssm_chunk_layout(zc, ctx_len), ssm_chunk_layout(z, seq_len), ssm_ops, batch)
        y_ssm_c, y_ssm = ssm_token_octets(y_chunks_c), ssm_token_octets(y_chunks)
        a_attn = diff_attention(z, [zc, z], lam, attn_subln_g[l], lam_init, batch)
        x1, h2, lg = merge_residual(y_ssm, z, a_conv, a_attn, x2, (lat[2], lat[4], lat[3]), wts, seq_len, alpha)
        sel, aff = route_select(lg, cap)
        sel4, aff4 = sel[:, :, None, :], aff[:, :, None, :]
        groups = [(gather_tokens(sel4, h2, cap), sel4, aff4, cap)]
        if ctx_out:
            a_conv_c = conv_branch(zc, ctx_len, conv_w[l], conv_b[l], conv_ln_g[l], conv_ln_b[l])
            a_attn_c = diff_attention(zc, [zc], lam, attn_subln_g[l], lam_init, batch)
            xc1, hc2, lgc = merge_residual(y_ssm_c, zc, a_conv_c, a_attn_c, xc2,
                                           (cxm[2], cxm[4], cxm[3]), wts, ctx_len, alpha)
            selc, affc = route_select(lgc, cap_c)
            selc4, affc4 = selc[:, :, None, :], affc[:, :, None, :]
            groups.append((gather_tokens(selc4, hc2, cap_c), selc4, affc4, cap_c))
        ye = expert_ffn(groups, w_gate_up, w_down, l, batch)
        x2 = combine_residual(jnp.swapaxes(sel, 1, 2), ye[0], x1, lat[5], row(ln2_g[l]), row(ln2_b[l]), cap, alpha)
        if ctx_out:
            xc2 = combine_residual(jnp.swapaxes(selc, 1, 2), ye[1], xc1, cxm[5], row(ln2_g[l]), row(ln2_b[l]),
                                   cap_c, alpha)
    return x2.reshape(batch, seq_len, d)
```

```python
import functools
import math

import jax
import jax.numpy as jnp
import numpy as np
from jax import lax
from jax.experimental import pallas as pl
from jax.experimental.pallas import tpu as pltpu

F32 = jnp.float32
BF16 = jnp.bfloat16

D_MODEL = 1024
GRID_W = 64
N_BRANCH = 3
CONV_WIDTH = 31
CONV_HALO = 16
SSM_GROUP = 16
SSM_GROUPS = D_MODEL // SSM_GROUP
SSM_STATE = 64
SSM_CHUNK = 16
SSM_ROWS = 8
GROUP_W = SSM_CHUNK * SSM_GROUP
SSM_GB = 2
ATTN_HEADS = 8
ATTN_DH = 64
ATTN_DV = 128
ROPE_BASE = 10000.0
N_EXPERTS = 16
D_EXPERT = 2816
CAPACITY_FACTOR = 2
LN_EPS = 1e-6
RMS_EPS = 1e-5
LANES = 128

COL_CONV_V, COL_CONV_G, COL_GATE0, COL_Q, COL_U, COL_K, COL_V = 0, 1, 2, 5, 6, 7, 8
N_COLS = 9

VMEM_LIMIT = 56 * 1024 * 1024


def _cparams(sem, vmem=VMEM_LIMIT):
    return pltpu.CompilerParams(dimension_semantics=sem, vmem_limit_bytes=vmem)


def _ln(x):
    mu = jnp.mean(x, axis=-1, keepdims=True)
    xc = x - mu
    var = jnp.mean(xc * xc, axis=-1, keepdims=True)
    return xc * lax.rsqrt(var + LN_EPS)


def _sigmoid(x):
    return 0.5 * jnp.tanh(0.5 * x) + 0.5


def _dot(a, b):
    return jnp.dot(a, b, preferred_element_type=F32)


def _dot_nt(a, b):
    return lax.dot_general(a, b, (((1,), (1,)), ((), ())), preferred_element_type=F32)


def _split_bf16(x):
    hi = x.astype(BF16)
    lo = (x - hi.astype(F32)).astype(BF16)
    return hi, lo


def _ada_kernel(c_ref, w_ref, b_ref, o_ref):
    c = c_ref[...]
    s = c * _sigmoid(c)
    hi, lo = _split_bf16(s)
    whi, wlo = _split_bf16(w_ref[...])
    o_ref[...] = _dot(hi, whi) + _dot(hi, wlo) + _dot(lo, whi) + b_ref[...]


def ada_modulation(cc, w_ada, b_ada):
    n = w_ada.shape[1]
    tn = 1024
    return pl.pallas_call(
        _ada_kernel,
        out_shape=jax.ShapeDtypeStruct((8, n), F32),
        grid=(n // tn,),
        in_specs=[pl.BlockSpec((8, D_MODEL), lambda j: (0, 0)),
                  pl.BlockSpec((D_MODEL, tn), lambda j: (0, j)),
                  pl.BlockSpec((1, tn), lambda j: (0, j))],
        out_specs=pl.BlockSpec((8, tn), lambda j: (0, j)),
        compiler_params=_cparams(("arbitrary",)),
        name="ada_modulation",
    )(cc, w_ada, b_ada.reshape(1, n))


def _inproj_kernel(x_ref, sc_ref, sh_ref, w_ref, cos_ref, sa_ref, sb_ref, o_ref, h_ref, *, rope):
    j = pl.program_id(1)

    @pl.when(j == 0)
    def _():
        h = _ln(x_ref[...]) * (1.0 + sc_ref[...]) + sh_ref[...]
        h_ref[...] = h.astype(BF16)

    acc = _dot(h_ref[...], w_ref[...])
    tn = acc.shape[1]

    def roped(scale):
        cos, sa, sb = cos_ref[...], sa_ref[...], sb_ref[...]
        for hh in range(tn // LANES):
            seg = acc[:, hh * LANES:(hh + 1) * LANES]
            r = seg * cos + pltpu.roll(seg, LANES - 16, 1) * sa + pltpu.roll(seg, 16, 1) * sb
            o_ref[:, hh * LANES:(hh + 1) * LANES] = (r * scale).astype(o_ref.dtype)

    is_q = j == COL_Q
    is_k = j == COL_K
    if rope:
        @pl.when(is_q)
        def _():
            roped(ATTN_DH ** -0.5)

        @pl.when(is_k)
        def _():
            roped(1.0)

        @pl.when(jnp.logical_not(is_q | is_k))
        def _():
            o_ref[...] = acc.astype(o_ref.dtype)
    else:
        @pl.when(is_q)
        def _():
            o_ref[...] = (acc * (ATTN_DH ** -0.5)).astype(o_ref.dtype)

        @pl.when(jnp.logical_not(is_q))
        def _():
            o_ref[...] = acc.astype(o_ref.dtype)


def input_projection(x2d, sc, sh, w_bf16, rope_tabs, seq_len, rope):
    m = x2d.shape[0]
    tm = min(1024, seq_len)
    tps = seq_len // tm
    nb = sc.shape[0]
    cos, sa, sb = rope_tabs
    mod_map = (lambda i, j: (i // tps, 0, 0)) if nb > 1 else (lambda i, j: (0, 0, 0))
    tab_map = lambda i, j: (i % tps, 0)
    return pl.pallas_call(
        functools.partial(_inproj_kernel, rope=rope),
        out_shape=jax.ShapeDtypeStruct((m, N_COLS * D_MODEL), BF16),
        grid=(m // tm, N_COLS),
        in_specs=[pl.BlockSpec((tm, D_MODEL), lambda i, j: (i, 0)),
                  pl.BlockSpec((None, 1, D_MODEL), mod_map),
                  pl.BlockSpec((None, 1, D_MODEL), mod_map),
                  pl.BlockSpec((D_MODEL, D_MODEL), lambda i, j: (0, j)),
                  pl.BlockSpec((tm, LANES), tab_map),
                  pl.BlockSpec((tm, LANES), tab_map),
                  pl.BlockSpec((tm, LANES), tab_map)],
        out_specs=pl.BlockSpec((tm, D_MODEL), lambda i, j: (i, j)),
        scratch_shapes=[pltpu.VMEM((tm, D_MODEL), BF16)],
        compiler_params=_cparams(("parallel", "arbitrary")),
        name="input_projection",
    )(x2d, sc, sh, w_bf16, cos, sa, sb)


def rope_tables(length):
    rows = length // GRID_W
    row = jnp.repeat(jnp.arange(rows), GRID_W)
    col = jnp.tile(jnp.arange(GRID_W), rows)
    n_freq = ATTN_DH // 4
    inv_freq = ROPE_BASE ** (-jnp.arange(n_freq, dtype=F32) / n_freq)
    ang = jnp.stack([row, col], -1).astype(F32)[:, :, None, None] * inv_freq
    ang = jnp.broadcast_to(ang, (length, 2, 2, n_freq)).reshape(length, ATTN_DH)
    cos, sin = jnp.cos(ang), jnp.sin(ang)
    first_half = (jnp.arange(ATTN_DH) % 32) < 16
    sa = jnp.where(first_half, -sin, 0.0)
    sb = jnp.where(first_half, 0.0, sin)
    tile2 = lambda t: jnp.concatenate([t, t], axis=-1)
    return tile2(cos), tile2(sa), tile2(sb)


def _conv_kernel(cur_ref, prev_ref, next_ref, w_ref, b_ref, g_ref, beta_ref, o_ref, gbuf, ybuf, *, tps):
    ti = pl.program_id(0) % tps
    tm = cur_ref.shape[0]
    d = D_MODEL

    def glu(ref):
        return ref[:, :d].astype(F32) * _sigmoid(ref[:, d:].astype(F32))

    gbuf[CONV_HALO:CONV_HALO + tm, :] = glu(cur_ref)
    gbuf[0:CONV_HALO, :] = jnp.where(ti > 0, glu(prev_ref), 0.0)
    gbuf[CONV_HALO + tm:, :] = jnp.where(ti < tps - 1, glu(next_ref), 0.0)
    off = CONV_HALO - CONV_WIDTH // 2
    rows = min(tm, 128)
    for c in range(d // LANES):
        cs = slice(c * LANES, (c + 1) * LANES)
        for r0 in range(0, tm, rows):
            acc = None
            for res in range(8):
                part = None
                for o in range(res, off + CONV_WIDTH, 8):
                    if o < off:
                        continue
                    term = gbuf[r0 + o - res:r0 + o - res + rows + 8, cs] * w_ref[o - off:o - off + 1, cs]
                    part = term if part is None else part + term
                part = part[res:res + rows, :]
                acc = part if acc is None else acc + part
            ybuf[r0:r0 + rows, cs] = acc
    y = _ln(ybuf[...] + b_ref[...]) * g_ref[...] + beta_ref[...]
    o_ref[...] = (y * _sigmoid(y)).astype(o_ref.dtype)


def conv_branch(z, seq_len, conv_w, conv_b, ln_g, ln_b):
    m = z.shape[0]
    tm = min(256, seq_len)
    tps = seq_len // tm
    hb = tm // CONV_HALO
    nblk = m // CONV_HALO
    row = lambda a: a.reshape(1, D_MODEL)
    return pl.pallas_call(
        functools.partial(_conv_kernel, tps=tps),
        out_shape=jax.ShapeDtypeStruct((m, D_MODEL), BF16),
        grid=(m // tm,),
        in_specs=[pl.BlockSpec((tm, 2 * D_MODEL), lambda i: (i, 0)),
                  pl.BlockSpec((CONV_HALO, 2 * D_MODEL), lambda i: (jnp.maximum(i * hb - 1, 0), 0)),
                  pl.BlockSpec((CONV_HALO, 2 * D_MODEL), lambda i: (jnp.minimum((i + 1) * hb, nblk - 1), 0)),
                  pl.BlockSpec((CONV_WIDTH, D_MODEL), lambda i: (0, 0)),
                  pl.BlockSpec((1, D_MODEL), lambda i: (0, 0)),
                  pl.BlockSpec((1, D_MODEL), lambda i: (0, 0)),
                  pl.BlockSpec((1, D_MODEL), lambda i: (0, 0))],
        out_specs=pl.BlockSpec((tm, D_MODEL), lambda i: (i, 0)),
        scratch_shapes=[pltpu.VMEM((tm + 2 * CONV_HALO, D_MODEL), F32), pltpu.VMEM((tm, D_MODEL), F32)],
        compiler_params=_cparams(("parallel",)),
        name="conv_branch",
    )(z, z, z, conv_w, row(conv_b), row(ln_g), row(ln_b))


def ssm_prepare(a_re, a_im, log_dt, b_re, b_im, c_re, c_im):
    a = lax.complex(a_re.astype(F32), a_im.astype(F32))
    dt_a = jnp.exp(log_dt.astype(F32))[..., None] * a
    b = lax.complex(b_re.astype(F32), b_im.astype(F32))
    b_bar = ((jnp.exp(dt_a) - 1.0) / a)[..., None] * b
    cc = lax.complex(c_re.astype(F32), c_im.astype(F32))
    n = SSM_CHUNK
    steps = jnp.arange(n + 1, dtype=F32)
    pw = jnp.exp(steps[None, None, :, None] * dt_a[:, :, None, :])
    exact = functools.partial(jnp.einsum, precision=lax.Precision.HIGHEST)
    lag_resp = jnp.real(exact('dgop,dgkp,dgpi->dgkio', cc, pw[:, :, :n], b_bar))
    idx = np.arange(n)
    lag = idx[None, :] - idx[:, None]
    pick_f = (lag[None] == idx[:, None, None]).astype(np.float32)
    pick_r = (-lag[None] == idx[:, None, None]).astype(np.float32)
    t_sum = (exact('kst,gkio->gsito', pick_f, lag_resp[0]) + exact('kst,gkio->gsito', pick_r, lag_resp[1])
             ).reshape(SSM_GROUPS, GROUP_W, GROUP_W)

    def parts(z, sign):
        return [jnp.real(z), sign * jnp.imag(z)]

    w_f = exact('gsp,gpi->gsip', pw[0][:, n - 1 - idx], b_bar[0]).reshape(SSM_GROUPS, GROUP_W, SSM_STATE)
    w_r = exact('gsp,gpi->gsip', pw[1][:, idx], b_bar[1]).reshape(SSM_GROUPS, GROUP_W, SSM_STATE)
    w_cat = jnp.concatenate(parts(w_f, 1.0) + parts(w_r, 1.0), axis=-1)
    v_f = exact('gop,gtp->gpto', cc[0], pw[0][:, 1 + idx]).reshape(SSM_GROUPS, SSM_STATE, GROUP_W)
    v_r = exact('gop,gtp->gpto', cc[1], pw[1][:, n - idx]).reshape(SSM_GROUPS, SSM_STATE, GROUP_W)
    v_cat = jnp.concatenate(parts(v_f, -1.0) + parts(v_r, -1.0), axis=1)
    a_n = pw[:, :, n]
    mul_same = jnp.concatenate([jnp.real(a_n[0])] * 2 + [jnp.real(a_n[1])] * 2, axis=-1)
    mul_swap = jnp.concatenate([-jnp.imag(a_n[0]), jnp.imag(a_n[0]), -jnp.imag(a_n[1]), jnp.imag(a_n[1])],
                               axis=-1)
    flat = lambda t: t.reshape(1, SSM_GROUPS * GROUP_W)
    return _split_bf16(t_sum) + _split_bf16(w_cat) + _split_bf16(v_cat) + (flat(mul_same), flat(mul_swap))


def _chunk_perm(octet_lanes):
    p = np.zeros((octet_lanes, octet_lanes), np.float32)
    for s in range(SSM_CHUNK):
        for gl in range(LANES // SSM_GROUP):
            for i in range(SSM_GROUP):
                p[s * LANES + gl * SSM_GROUP + i, gl * GROUP_W + s * SSM_GROUP + i] = 1.0
    return p


def _ssm_in_kernel(x_ref, perm_ref, u2_ref, xf):
    rb = x_ref.shape[0]
    xf[...] = x_ref[...].reshape(rb * SSM_CHUNK, LANES).astype(F32)
    lhs = jnp.concatenate([xf[pl.ds(s, rb, stride=SSM_CHUNK), :] for s in range(SSM_CHUNK)], axis=1)
    u2_ref[...] = _dot(lhs.astype(BF16), perm_ref[...]).astype(u2_ref.dtype)


def ssm_chunk_layout(z):
    rows = z.shape[0] // SSM_CHUNK
    z3 = z.reshape(rows, SSM_CHUNK, N_COLS * D_MODEL)
    rb = min(256, rows)
    octets = D_MODEL // LANES
    ow = octets * GROUP_W
    perm = jnp.asarray(_chunk_perm(ow), BF16)
    return pl.pallas_call(
        _ssm_in_kernel,
        out_shape=jax.ShapeDtypeStruct((rows, SSM_GROUPS * GROUP_W), BF16),
        grid=(rows // rb, octets),
        in_specs=[pl.BlockSpec((rb, SSM_CHUNK, LANES), lambda r, j: (r, 0, COL_U * octets + j)),
                  pl.BlockSpec((ow, ow), lambda r, j: (0, 0))],
        out_specs=pl.BlockSpec((rb, ow), lambda r, j: (r, j)),
        scratch_shapes=[pltpu.VMEM((rb * SSM_CHUNK, LANES), F32)],
        compiler_params=_cparams(("parallel", "arbitrary")),
        name="ssm_chunk_layout",
    )(z3, perm)


def _ssm_out_kernel(y_ref, perm_ref, o_ref):
    y_hi, y_lo = _split_bf16(y_ref[...])
    o_ref[...] = _dot(y_hi, perm_ref[...]) + _dot(y_lo, perm_ref[...])


def ssm_token_octets(y2):
    rows = y2.shape[0]
    rb = min(256, rows)
    octets = D_MODEL // LANES
    ow = octets * GROUP_W
    perm_t = jnp.asarray(_chunk_perm(ow).T, BF16)
    blk = pl.BlockSpec((rb, ow), lambda r, j: (r, j))
    return pl.pallas_call(
        _ssm_out_kernel,
        out_shape=jax.ShapeDtypeStruct(y2.shape, F32),
        grid=(rows // rb, octets),
        in_specs=[blk, pl.BlockSpec((ow, ow), lambda r, j: (0, 0))],
        out_specs=blk,
        compiler_params=_cparams(("parallel", "arbitrary")),
        name="ssm_token_octets",
    )(y2, perm_t)


def _ssm_kernel(uc_ref, ul_ref, t_hi_ref, t_lo_ref, w_hi_ref, w_lo_ref, v_hi_ref, v_lo_ref, same_ref, swap_ref,
                yc_ref, yl_ref, sbuf, xbuf, *, batch):
    ncol, srows, _ = sbuf.shape
    nch = srows // SSM_ROWS
    n_ctx = uc_ref.shape[0] // batch
    n_lat = ul_ref.shape[0] // batch
    groups = t_hi_ref.shape[0]
    gsl = lambda k: slice(k * GROUP_W, (k + 1) * GROUP_W)

    def chunk_rows(b, first, count):
        return pl.ds(first * SSM_ROWS + b, count, stride=SSM_ROWS)

    sbuf[...] = jnp.zeros(sbuf.shape, F32)
    for u_ref, first, count in ((uc_ref, 0, n_ctx), (ul_ref, n_ctx, n_lat)):
        for k in range(groups):
            u = u_ref[:, gsl(k)]
            inc = _dot(u, w_hi_ref[k]) + _dot(u, w_lo_ref[k])
            for v in range(2):
                for b in range(batch):
                    sbuf[2 * k + v, chunk_rows(b, first, count), :] = inc[b * count:(b + 1) * count,
                                                                          v * LANES:(v + 1) * LANES]
    for v in range(ncol):
        xbuf[v] = pltpu.roll(sbuf[v], LANES // 2, 1)

    same = [jnp.broadcast_to(same_ref[:, v * LANES:(v + 1) * LANES], (SSM_ROWS, LANES)) for v in range(ncol)]
    swap = [jnp.broadcast_to(swap_ref[:, v * LANES:(v + 1) * LANES], (SSM_ROWS, LANES)) for v in range(ncol)]

    def body(j, state):
        cr = jnp.where(j < n_ctx, n_ctx - 1 - j, nch - 1 - (j - n_ctx))
        new = []
        for v in range(ncol):
            ci = j if v % 2 == 0 else cr
            rs = pl.ds(pl.multiple_of(ci * SSM_ROWS, SSM_ROWS), SSM_ROWS)
            s, sx = state[2 * v], state[2 * v + 1]
            inc = sbuf[v, rs, :]
            sbuf[v, rs, :] = s
            new.append(s * same[v] + sx * swap[v] + inc)
            new.append(sx * same[v] - s * swap[v] + xbuf[v, rs, :])
        return tuple(new)

    lax.fori_loop(0, nch, body, tuple(jnp.zeros((SSM_ROWS, LANES), F32) for _ in range(2 * ncol)), unroll=4)

    for u_ref, y_ref, first, count in ((uc_ref, yc_ref, 0, n_ctx), (ul_ref, yl_ref, n_ctx, n_lat)):
        for k in range(groups):
            u = u_ref[:, gsl(k)]
            st = jnp.concatenate(
                [jnp.concatenate([sbuf[2 * k + v, chunk_rows(b, first, count), :] for b in range(batch)], axis=0)
                 for v in range(2)], axis=1)
            s_hi, s_lo = _split_bf16(st)
            y_ref[:, gsl(k)] = (_dot(u, t_hi_ref[k]) + _dot(u, t_lo_ref[k]) + _dot(s_hi, v_hi_ref[k])
                                + _dot(s_hi, v_lo_ref[k]) + _dot(s_lo, v_hi_ref[k]))


def ssm_mixer(u2c, u2l, ops, batch):
    t_hi, t_lo, w_hi, w_lo, v_hi, v_lo, mul_same, mul_swap = ops
    gw = SSM_GB * GROUP_W
    nch = (u2c.shape[0] + u2l.shape[0]) // batch
    col = lambda a: pl.BlockSpec((a.shape[0], gw), lambda j: (0, j))
    wsp = pl.BlockSpec((SSM_GB, GROUP_W, GROUP_W), lambda j: (j, 0, 0))
    tab = pl.BlockSpec((1, gw), lambda j: (0, j))
    return pl.pallas_call(
        functools.partial(_ssm_kernel, batch=batch),
        out_shape=(jax.ShapeDtypeStruct(u2c.shape, F32), jax.ShapeDtypeStruct(u2l.shape, F32)),
        grid=(SSM_GROUPS // SSM_GB,),
        in_specs=[col(u2c), col(u2l)] + [wsp] * 6 + [tab, tab],
        out_specs=(col(u2c), col(u2l)),
        scratch_shapes=[pltpu.VMEM((gw // LANES, nch * SSM_ROWS, LANES), F32)] * 2,
        compiler_params=_cparams(("parallel",)),
        name="ssm_mixer",
    )(u2c, u2l, t_hi, t_lo, w_hi, w_lo, v_hi, v_lo, mul_same, mul_swap)


def _attn_kernel(*refs, n_src, lam_init, key_block):
    lam_ref, g_ref, q_ref = refs[:3]
    k_refs = refs[3:3 + n_src]
    v_refs = refs[3 + n_src:3 + 2 * n_src]
    o_ref, vext = refs[3 + 2 * n_src], refs[4 + 2 * n_src]
    qi = pl.program_id(2)

    @pl.when(qi == 0)
    def _():
        r0 = 0
        for v_ref in v_refs:
            n = v_ref.shape[0]
            vext[r0:r0 + n, :ATTN_DV] = v_ref[...]
            vext[r0:r0 + n, ATTN_DV:] = jnp.ones((n, ATTN_DV), BF16)
            r0 += n

    q = q_ref[...]
    lane = lax.broadcasted_iota(jnp.int32, q.shape, 1)
    outs = []
    for sub in range(2):
        qs = jnp.where((lane >= ATTN_DH) if sub else (lane < ATTN_DH), q, jnp.zeros_like(q))
        m = acc = None
        r0 = 0
        for k_ref in k_refs:
            n = k_ref.shape[0]
            for c0 in range(0, n, key_block):
                c1 = min(c0 + key_block, n)
                s = _dot_nt(qs, k_ref[c0:c1, :])
                bm = jnp.max(s, axis=-1, keepdims=True)
                m_new = bm if m is None else jnp.maximum(m, bm)
                p = jnp.exp((s - m_new).astype(BF16))
                pv = _dot(p, vext[r0 + c0:r0 + c1, :])
                acc = pv if acc is None else acc * jnp.exp(m - m_new) + pv
                m = m_new
            r0 += n
        outs.append(acc[:, :ATTN_DV] / acc[:, ATTN_DV:])
    o = outs[0] - lam_ref[0] * outs[1]
    o = o * lax.rsqrt(jnp.mean(o * o, axis=-1, keepdims=True) + RMS_EPS) * g_ref[...]
    o_ref[...] = (o * (1.0 - lam_init)).astype(o_ref.dtype)


def diff_attention(zq, key_srcs, lam, subln_g, lam_init, batch):
    lq = zq.shape[0] // batch
    tq = min(1024, lq)
    nq = lq // tq
    lks = [k.shape[0] // batch for k in key_srcs]
    n_src = len(key_srcs)
    hpb = D_MODEL // LANES
    in_specs = [pl.BlockSpec(memory_space=pltpu.SMEM),
                pl.BlockSpec((1, ATTN_DV), lambda b, h, i: (0, 0)),
                pl.BlockSpec((tq, LANES), lambda b, h, i: (b * nq + i, COL_Q * hpb + h))]
    in_specs += [pl.BlockSpec((lk, LANES), lambda b, h, i: (b, COL_K * hpb + h)) for lk in lks]
    in_specs += [pl.BlockSpec((lk, LANES), lambda b, h, i: (b, COL_V * hpb + h)) for lk in lks]
    return pl.pallas_call(
        functools.partial(_attn_kernel, n_src=n_src, lam_init=lam_init, key_block=256),
        out_shape=jax.ShapeDtypeStruct((batch * lq, D_MODEL), BF16),
        grid=(batch, ATTN_HEADS, nq),
        in_specs=in_specs,
        out_specs=pl.BlockSpec((tq, LANES), lambda b, h, i: (b * nq + i, h)),
        scratch_shapes=[pltpu.VMEM((sum(lks), 2 * ATTN_DV), BF16)],
        compiler_params=_cparams(("parallel", "parallel", "arbitrary")),
        name="diff_attention",
    )(lam.reshape(1), subln_g.reshape(1, ATTN_DV), zq, *key_srcs, *key_srcs)


def _merge_kernel(y_ref, u_ref, ac_ref, aa_ref, zg0_ref, zg1_ref, zg2_ref, x_ref,
                  d_ref, wglu_ref, ws_ref, wc_ref, wa_ref, wo_ref, g1_ref, sc2_ref, sh2_ref,
                  lng_ref, lnb_ref, wr_hi_ref, wr_lo_ref, x1_ref, h2_ref, lg_ref, ybuf, *, alpha):
    chunks = y_ref.shape[0]
    for j in range(D_MODEL // LANES):
        for t in range(SSM_CHUNK):
            p0 = (j * SSM_CHUNK + t) * LANES
            ybuf[j, pl.ds(t, chunks, stride=SSM_CHUNK), :] = y_ref[:, p0:p0 + LANES]
    y_ssm_raw = jnp.concatenate([ybuf[j] for j in range(D_MODEL // LANES)], axis=1)
    y = y_ssm_raw + d_ref[...] * u_ref[...].astype(F32)
    g = 0.5 * y * (1.0 + jnp.tanh(math.sqrt(2.0 / math.pi) * (y + 0.044715 * (y * y * y))))
    a_ssm = g * _sigmoid(_dot(g.astype(BF16), wglu_ref[...]))
    y_ssm = _dot(a_ssm.astype(BF16), ws_ref[...])
    y_conv = _dot(ac_ref[...], wc_ref[...])
    y_attn = _dot(aa_ref[...], wa_ref[...])
    m = (_sigmoid(zg0_ref[...].astype(F32)) * y_conv + _sigmoid(zg1_ref[...].astype(F32)) * y_ssm
         + _sigmoid(zg2_ref[...].astype(F32)) * y_attn)
    yo = _dot(m.astype(BF16), wo_ref[...])
    x1 = _ln(alpha * x_ref[...] + g1_ref[...] * yo) * lng_ref[...] + lnb_ref[...]
    x1_ref[...] = x1
    h2 = _ln(x1) * (1.0 + sc2_ref[...]) + sh2_ref[...]
    h2_ref[...] = h2.astype(BF16)
    hi, lo = _split_bf16(h2)
    lg_ref[...] = _dot_nt(wr_hi_ref[...], hi) + _dot_nt(wr_hi_ref[...], lo) + _dot_nt(wr_lo_ref[...], hi)


def merge_residual(y_ssm, z, a_conv, a_attn, x2d, mods, wts, seq_len, alpha):
    m = x2d.shape[0]
    batch = m // seq_len
    tm = min(512, seq_len)
    tps = seq_len // tm
    g1, sc2, sh2 = mods
    nb = g1.shape[0]
    mod_map = (lambda i: (i // tps, 0, 0)) if nb > 1 else (lambda i: (0, 0, 0))
    tile = pl.BlockSpec((tm, D_MODEL), lambda i: (i, 0))
    zcol = lambda cidx: pl.BlockSpec((tm, D_MODEL), lambda i: (i, cidx))
    modspec = pl.BlockSpec((None, 1, D_MODEL), mod_map)
    rowspec = pl.BlockSpec((1, D_MODEL), lambda i: (0, 0))
    wspec = pl.BlockSpec((D_MODEL, D_MODEL), lambda i: (0, 0))
    wrspec = pl.BlockSpec((N_EXPERTS, D_MODEL), lambda i: (0, 0))
    return pl.pallas_call(
        functools.partial(_merge_kernel, alpha=alpha),
        out_shape=(jax.ShapeDtypeStruct((m, D_MODEL), F32), jax.ShapeDtypeStruct((m, D_MODEL), BF16),
                   jax.ShapeDtypeStruct((batch, N_EXPERTS, seq_len), F32)),
        grid=(m // tm,),
        in_specs=[pl.BlockSpec((tm // SSM_CHUNK, SSM_GROUPS * GROUP_W), lambda i: (i, 0)),
                  zcol(COL_U), tile, tile, zcol(COL_GATE0), zcol(COL_GATE0 + 1), zcol(COL_GATE0 + 2),
                  tile, rowspec, wspec, wspec, wspec, wspec, wspec, modspec, modspec, modspec,
                  rowspec, rowspec, wrspec, wrspec],
        out_specs=(tile, tile, pl.BlockSpec((None, N_EXPERTS, tm), lambda i: (i // tps, 0, i % tps))),
        scratch_shapes=[pltpu.VMEM((D_MODEL // LANES, tm, LANES), F32)],
        compiler_params=_cparams(("parallel",)),
        name="merge_residual",
    )(y_ssm, z, a_conv, a_attn, z, z, z, x2d, wts['ssm_d'], wts['w_ssm_glu'], wts['w_ssm_out'],
      wts['w_conv_out'], wts['w_attn_out'], wts['w_o'], g1, sc2, sh2, wts['ln1_g'], wts['ln1_b'],
      wts['wr_hi'], wts['wr_lo'])


def _select_kernel(lg_ref, tri_ref, sel_ref, aff_ref, csum, *, cap):
    lg = lg_ref[...]
    n = lg.shape[1]
    e = jnp.exp(lg - jnp.max(lg, axis=0, keepdims=True))
    aff = e / jnp.sum(e, axis=0, keepdims=True)
    aff_ref[...] = aff
    bits = pltpu.bitcast(aff, jnp.int32)
    capf = float(cap)

    def count(mask):
        return jnp.sum(jnp.where(mask, 1.0, 0.0), axis=1, keepdims=True)

    thr = jnp.zeros((lg.shape[0], 1), jnp.int32)
    for bit in range(30, -1, -1):
        cand = thr | (1 << bit)
        thr = jnp.where(count(bits >= cand) >= capf, cand, thr)
    gt = bits > thr
    eq = bits == thr
    need = capf - count(gt)

    def excl_cumsum(mask):
        x = jnp.where(mask, 1.0, 0.0)
        carry = jnp.zeros((lg.shape[0], 1), F32)
        for j in range(n // LANES):
            blk = x[:, j * LANES:(j + 1) * LANES]
            inc = _dot(blk.astype(BF16), tri_ref[...])
            csum[:, j * LANES:(j + 1) * LANES] = inc - blk + carry
            carry = carry + inc[:, LANES - 1:LANES]
        return csum[...]

    mask = gt | (eq & (excl_cumsum(eq) < need))
    pos = excl_cumsum(mask)
    sel_ref[...] = jnp.where(mask, pos, -1.0)


def route_select(logits_t, cap):
    batch, n_e, n = logits_t.shape
    tri = jnp.asarray(np.triu(np.ones((LANES, LANES), np.float32)), BF16)
    spec = pl.BlockSpec((None, n_e, n), lambda b: (b, 0, 0))
    shp = jax.ShapeDtypeStruct((batch, n_e, n), F32)
    return pl.pallas_call(
        functools.partial(_select_kernel, cap=cap),
        out_shape=(shp, shp),
        grid=(batch,),
        in_specs=[spec, pl.BlockSpec((LANES, LANES), lambda b: (0, 0))],
        out_specs=(spec, spec),
        scratch_shapes=[pltpu.VMEM((n_e, n), F32)],
        compiler_params=_cparams(("parallel",)),
        name="route_select",
    )(logits_t, tri)


def _gather_kernel(sel_ref, h_ref, xs_ref):
    cap = xs_ref.shape[0]
    slot = lax.broadcasted_iota(jnp.int32, (cap, 1), 0).astype(F32)
    onehot = jnp.where(sel_ref[...] == slot, 1.0, 0.0).astype(BF16)
    xs_ref[...] = _dot(onehot, h_ref[...]).astype(xs_ref.dtype)


def gather_tokens(sel4, h2, cap):
    batch, n_e, _, n = sel4.shape
    return pl.pallas_call(
        _gather_kernel,
        out_shape=jax.ShapeDtypeStruct((n_e, batch * cap, D_MODEL), BF16),
        grid=(batch, n_e),
        in_specs=[pl.BlockSpec((None, None, 1, n), lambda b, e: (b, e, 0, 0)),
                  pl.BlockSpec((n, D_MODEL), lambda b, e: (b, 0))],
        out_specs=pl.BlockSpec((None, cap, D_MODEL), lambda b, e: (e, b, 0)),
        compiler_params=_cparams(("parallel", "arbitrary")),
        name="gather_tokens",
    )(sel4, h2)


def _experts_kernel(*refs, n_grp, caps, batch):
    wg_ref, wu_ref, wd_ref = refs[:3]
    xs_refs = refs[3:3 + n_grp]
    sel_refs = refs[3 + n_grp:3 + 2 * n_grp]
    aff_refs = refs[3 + 2 * n_grp:3 + 3 * n_grp]
    ye_refs = refs[3 + 3 * n_grp:3 + 4 * n_grp]
    acc_refs = refs[3 + 4 * n_grp:]
    f = pl.program_id(1)
    nf = pl.num_programs(1)
    @pl.when(f == 0)
    def _():
        for acc_ref in acc_refs:
            acc_ref[...] = jnp.zeros(acc_ref.shape, F32)

    wg = wg_ref[...].astype(BF16)
    wu = wu_ref[...].astype(BF16)
    wd = wd_ref[...].astype(BF16)
    for xs_ref, acc_ref in zip(xs_refs, acc_refs):
        rows = xs_ref.shape[0]
        step = min(512, rows)
        for r0 in range(0, rows, step):
            x = xs_ref[r0:r0 + step, :]
            a = _dot(x, wg)
            act = (a * _sigmoid(a) * _dot(x, wu)).astype(BF16)
            acc_ref[r0:r0 + step, :] += _dot(act, wd)

    @pl.when(f == nf - 1)
    def _():
        for sel_ref, aff_ref, ye_ref, acc_ref, cap in zip(sel_refs, aff_refs, ye_refs, acc_refs, caps):
            slot = lax.broadcasted_iota(jnp.int32, (cap, 1), 0).astype(F32)
            for b in range(batch):
                gate = jnp.sum(jnp.where(sel_ref[b] == slot, aff_ref[b], 0.0), axis=1, keepdims=True)
                rs = slice(b * cap, (b + 1) * cap)
                ye_ref[rs, :] = (acc_ref[rs, :] * gate).astype(ye_ref.dtype)


def expert_ffn(groups, w_gate_up, w_down, layer, batch):
    tf = 256
    nf = D_EXPERT // tf
    n_grp = len(groups)
    caps = tuple(g[3] for g in groups)
    xs_specs = [pl.BlockSpec((None, g[0].shape[1], D_MODEL), lambda e, f: (e, 0, 0)) for g in groups]
    sa_specs = [pl.BlockSpec((batch, None, 1, g[1].shape[3]), lambda e, f: (0, e, 0, 0)) for g in groups]
    return pl.pallas_call(
        functools.partial(_experts_kernel, n_grp=n_grp, caps=caps, batch=batch),
        out_shape=tuple(jax.ShapeDtypeStruct(g[0].shape, BF16) for g in groups),
        grid=(N_EXPERTS, nf),
        in_specs=[pl.BlockSpec((None, None, D_MODEL, tf), lambda e, f: (layer, e, 0, f)),
                  pl.BlockSpec((None, None, D_MODEL, tf), lambda e, f: (layer, e, 0, nf + f)),
                  pl.BlockSpec((None, None, tf, D_MODEL), lambda e, f: (layer, e, f, 0))]
        + xs_specs + sa_specs + sa_specs,
        out_specs=tuple(xs_specs),
        scratch_shapes=[pltpu.VMEM(g[0].shape[1:], F32) for g in groups],
        compiler_params=_cparams(("parallel", "arbitrary")),
        name="expert_ffn",
    )(w_gate_up, w_gate_up, w_down, *[g[0] for g in groups], *[g[1] for g in groups], *[g[2] for g in groups])


def _combine_kernel(selt_ref, ye_ref, x1_ref, g2_ref, lng_ref, lnb_ref, o_ref, *, alpha):
    n_e, cap, _ = ye_ref.shape
    st = selt_ref[...]
    slot = lax.broadcasted_iota(jnp.int32, (1, cap), 1).astype(F32)
    acc = None
    for e in range(n_e):
        onehot = jnp.where(st[:, e:e + 1] == slot, 1.0, 0.0).astype(BF16)
        part = _dot(onehot, ye_ref[e])
        acc = part if acc is None else acc + part
    o_ref[...] = _ln(alpha * x1_ref[...] + g2_ref[...] * acc) * lng_ref[...] + lnb_ref[...]


def combine_residual(sel_t, ye, x1, g2, ln_g, ln_b, cap, alpha):
    batch, n, n_e = sel_t.shape
    tn = min(512, n)
    nt = n // tn
    nb = g2.shape[0]
    mod_map = (lambda b, t: (b, 0, 0)) if nb > 1 else (lambda b, t: (0, 0, 0))
    rowspec = pl.BlockSpec((1, D_MODEL), lambda b, t: (0, 0))
    tile = pl.BlockSpec((tn, D_MODEL), lambda b, t: (b * nt + t, 0))
    return pl.pallas_call(
        functools.partial(_combine_kernel, alpha=alpha),
        out_shape=jax.ShapeDtypeStruct((batch * n, D_MODEL), F32),
        grid=(batch, nt),
        in_specs=[pl.BlockSpec((None, tn, n_e), lambda b, t: (b, t, 0)),
                  pl.BlockSpec((n_e, cap, D_MODEL), lambda b, t: (0, b, 0)),
                  tile, pl.BlockSpec((None, 1, D_MODEL), mod_map), rowspec, rowspec],
        out_specs=tile,
        compiler_params=_cparams(("parallel", "arbitrary")),
        name="combine_residual",
    )(sel_t, ye, x1, g2, ln_g, ln_b)


def kernel(x, c, ctx, c_ctx, w_ada, b_ada, w_in, conv_w, conv_b, conv_ln_g, conv_ln_b, w_conv_out, ssm_a_re, ssm_a_im, ssm_log_dt, ssm_b_re, ssm_b_im, ssm_c_re, ssm_c_im, ssm_d, w_ssm_glu, w_ssm_out, attn_lambda, attn_subln_g, w_attn_out, w_o, ln1_g, ln1_b, w_router, w_gate_up, w_down, ln2_g, ln2_b):
    batch, seq_len, d = x.shape
    ctx_len = ctx.shape[1]
    depth = w_in.shape[0]
    alpha = (2.0 * depth) ** 0.25
    cap = CAPACITY_FACTOR * seq_len // N_EXPERTS
    cap_c = CAPACITY_FACTOR * ctx_len // N_EXPERTS
    tabs = rope_tables(seq_len)
    tabs_c = tuple(t[:ctx_len] for t in tabs)
    row = lambda a: a.reshape(1, d)
    cc = jnp.zeros((8, d), F32).at[:batch].set(c).at[batch].set(c_ctx)

    x2 = x.reshape(batch * seq_len, d)
    xc2 = ctx.reshape(batch * ctx_len, d)
    for l in range(depth):
        ctx_out = l < depth - 1
        lam_init = 0.8 - 0.6 * math.exp(-0.3 * l)
        mod = ada_modulation(cc, w_ada[l], b_ada[l])
        part = lambda rows, k: mod[rows, k * d:(k + 1) * d][:, None, :]
        lat = [part(slice(0, batch), k) for k in range(6)]
        cxm = [part(slice(batch, batch + 1), k) for k in range(6)]
        w_in_b = w_in[l].astype(BF16)
        wr_hi, wr_lo = _split_bf16(w_router[l].T)
        wts = {'ssm_d': row(ssm_d[l]), 'w_ssm_glu': w_ssm_glu[l].astype(BF16), 'w_ssm_out': w_ssm_out[l].astype(BF16),
               'w_conv_out': w_conv_out[l].astype(BF16), 'w_attn_out': w_attn_out[l].astype(BF16),
               'w_o': w_o[l].astype(BF16), 'ln1_g': row(ln1_g[l]), 'ln1_b': row(ln1_b[l]),
               'wr_hi': wr_hi, 'wr_lo': wr_lo}
        lq1, lk1, lq2, lk2 = [attn_lambda[l, i].astype(F32) for i in range(4)]
        lam = jnp.exp(jnp.sum(lq1 * lk1)) - jnp.exp(jnp.sum(lq2 * lk2)) + lam_init
        ssm_ops = ssm_prepare(ssm_a_re[l], ssm_a_im[l], ssm_log_dt[l], ssm_b_re[l], ssm_b_im[l],
                              ssm_c_re[l], ssm_c_im[l])

        z = input_projection(x2, lat[1], lat[0], w_in_b, tabs, seq_len, rope=True)
        zc = input_projection(xc2, cxm[1], cxm[0], w_in_b, tabs_c, ctx_len, rope=False)

        a_conv = conv_branch(z, seq_len, conv_w[l], conv_b[l], conv_ln_g[l], conv_ln_b[l])
        y_chunks_c, y_chunks = ssm_mixer(ssm_chunk_layout(zc), ssm_chunk_layout(z), ssm_ops, batch)
        y_ssm_c, y_ssm = ssm_token_octets(y_chunks_c), ssm_token_octets(y_chunks)
        a_attn = diff_attention(z, [zc, z], lam, attn_subln_g[l], lam_init, batch)
        x1, h2, lg = merge_residual(y_ssm, z, a_conv, a_attn, x2, (lat[2], lat[4], lat[3]), wts, seq_len, alpha)
        sel, aff = route_select(lg, cap)
        sel4, aff4 = sel[:, :, None, :], aff[:, :, None, :]
        groups = [(gather_tokens(sel4, h2, cap), sel4, aff4, cap)]
        if ctx_out:
            a_conv_c = conv_branch(zc, ctx_len, conv_w[l], conv_b[l], conv_ln_g[l], conv_ln_b[l])
            a_attn_c = diff_attention(zc, [zc], lam, attn_subln_g[l], lam_init, batch)
            xc1, hc2, lgc = merge_residual(y_ssm_c, zc, a_conv_c, a_attn_c, xc2,
                                           (cxm[2], cxm[4], cxm[3]), wts, ctx_len, alpha)
            selc, affc = route_select(lgc, cap_c)
            selc4, affc4 = selc[:, :, None, :], affc[:, :, None, :]
            groups.append((gather_tokens(selc4, hc2, cap_c), selc4, affc4, cap_c))
        ye = expert_ffn(groups, w_gate_up, w_down, l, batch)
        x2 = combine_residual(jnp.swapaxes(sel, 1, 2), ye[0], x1, lat[5], row(ln2_g[l]), row(ln2_b[l]), cap, alpha)
        if ctx_out:
            xc2 = combine_residual(jnp.swapaxes(selc, 1, 2), ye[1], xc1, cxm[5], row(ln2_g[l]), row(ln2_b[l]),
                                   cap_c, alpha)
    return x2.reshape(batch, seq_len, d)
```

```python
import functools
import math

import jax
import jax.numpy as jnp
import numpy as np
from jax import lax
from jax.experimental import pallas as pl
from jax.experimental.pallas import tpu as pltpu

F32 = jnp.float32
BF16 = jnp.bfloat16

D_MODEL = 1024
GRID_W = 64
N_BRANCH = 3
CONV_WIDTH = 31
CONV_HALO = 16
SSM_GROUP = 16
SSM_GROUPS = D_MODEL // SSM_GROUP
SSM_STATE = 64
SSM_CHUNK = 16
SSM_ROWS = 8
GROUP_W = SSM_CHUNK * SSM_GROUP
SSM_GB = 2
ATTN_HEADS = 8
ATTN_DH = 64
ATTN_DV = 128
ROPE_BASE = 10000.0
N_EXPERTS = 16
D_EXPERT = 2816
CAPACITY_FACTOR = 2
LN_EPS = 1e-6
RMS_EPS = 1e-5
LANES = 128

COL_CONV_V, COL_CONV_G, COL_GATE0, COL_Q, COL_U, COL_K, COL_V = 0, 1, 2, 5, 6, 7, 8
N_COLS = 9

VMEM_LIMIT = 56 * 1024 * 1024


def _cparams(sem, vmem=VMEM_LIMIT):
    return pltpu.CompilerParams(dimension_semantics=sem, vmem_limit_bytes=vmem)


def _ln(x):
    mu = jnp.mean(x, axis=-1, keepdims=True)
    xc = x - mu
    var = jnp.mean(xc * xc, axis=-1, keepdims=True)
    return xc * lax.rsqrt(var + LN_EPS)


def _sigmoid(x):
    return 0.5 * jnp.tanh(0.5 * x) + 0.5


def _dot(a, b):
    return jnp.dot(a, b, preferred_element_type=F32)


def _dot_nt(a, b):
    return lax.dot_general(a, b, (((1,), (1,)), ((), ())), preferred_element_type=F32)


def _split_bf16(x):
    hi = x.astype(BF16)
    lo = (x - hi.astype(F32)).astype(BF16)
    return hi, lo


def _ada_kernel(c_ref, w_ref, b_ref, o_ref):
    c = c_ref[...]
    s = c * _sigmoid(c)
    hi, lo = _split_bf16(s)
    whi, wlo = _split_bf16(w_ref[...])
    o_ref[...] = _dot(hi, whi) + _dot(hi, wlo) + _dot(lo, whi) + b_ref[...]


def ada_modulation(cc, w_ada, b_ada):
    n = w_ada.shape[1]
    tn = 1024
    return pl.pallas_call(
        _ada_kernel,
        out_shape=jax.ShapeDtypeStruct((8, n), F32),
        grid=(n // tn,),
        in_specs=[pl.BlockSpec((8, D_MODEL), lambda j: (0, 0)),
                  pl.BlockSpec((D_MODEL, tn), lambda j: (0, j)),
                  pl.BlockSpec((1, tn), lambda j: (0, j))],
        out_specs=pl.BlockSpec((8, tn), lambda j: (0, j)),
        compiler_params=_cparams(("arbitrary",)),
        name="ada_modulation",
    )(cc, w_ada, b_ada.reshape(1, n))


def _inproj_kernel(x_ref, sc_ref, sh_ref, w_ref, *rest, rotary):
    o_ref, h_ref = rest[-2:]
    j = pl.program_id(1)

    @pl.when(j == 0)
    def _():
        h = _ln(x_ref[...]) * (1.0 + sc_ref[...]) + sh_ref[...]
        h_ref[...] = h.astype(BF16)

    acc = _dot(h_ref[...], w_ref[...])
    if rotary:
        cos, sa, sb = (r[...] for r in rest[:3])
        for hh in range(acc.shape[1] // LANES):
            seg = acc[:, hh * LANES:(hh + 1) * LANES]
            r = seg * cos + pltpu.roll(seg, LANES - 16, 1) * sa + pltpu.roll(seg, 16, 1) * sb
            o_ref[:, hh * LANES:(hh + 1) * LANES] = r.astype(o_ref.dtype)
    else:
        o_ref[...] = acc.astype(o_ref.dtype)


def input_projection(x2d, sc, sh, w_bf16, seq_len, rope_tabs=None):
    m = x2d.shape[0]
    tm = min(1024, seq_len)
    tps = seq_len // tm
    nb = sc.shape[0]
    rotary = rope_tabs is not None
    if rotary:
        n_out = 2
        w_col = lambda j: COL_Q + (COL_K - COL_Q) * j
        o_col = lambda j: j
    else:
        n_out = N_COLS - 2
        w_col = lambda j: j + jnp.where(j >= COL_Q, 1, 0) + jnp.where(j >= COL_K - 1, 1, 0)
        o_col = w_col
    mod_map = (lambda i, j: (i // tps, 0, 0)) if nb > 1 else (lambda i, j: (0, 0, 0))
    tab_spec = pl.BlockSpec((None, tm, LANES), lambda i, j: (j, i % tps, 0))
    return pl.pallas_call(
        functools.partial(_inproj_kernel, rotary=rotary),
        out_shape=jax.ShapeDtypeStruct((m, (2 if rotary else N_COLS) * D_MODEL), BF16),
        grid=(m // tm, n_out),
        in_specs=[pl.BlockSpec((tm, D_MODEL), lambda i, j: (i, 0)),
                  pl.BlockSpec((None, 1, D_MODEL), mod_map),
                  pl.BlockSpec((None, 1, D_MODEL), mod_map),
                  pl.BlockSpec((D_MODEL, D_MODEL), lambda i, j: (0, w_col(j)))] + ([tab_spec] * 3 if rotary else []),
        out_specs=pl.BlockSpec((tm, D_MODEL), lambda i, j: (i, o_col(j))),
        scratch_shapes=[pltpu.VMEM((tm, D_MODEL), BF16)],
        compiler_params=_cparams(("parallel", "arbitrary")),
        name="qk_projection" if rotary else "input_projection",
    )(x2d, sc, sh, w_bf16, *(rope_tabs if rotary else ()))


def rope_tables(length, rotate):
    rows = length // GRID_W
    row = jnp.repeat(jnp.arange(rows), GRID_W)
    col = jnp.tile(jnp.arange(GRID_W), rows)
    n_freq = ATTN_DH // 4
    inv_freq = ROPE_BASE ** (-jnp.arange(n_freq, dtype=F32) / n_freq)
    ang = jnp.stack([row, col], -1).astype(F32)[:, :, None, None] * inv_freq
    ang = jnp.broadcast_to(ang, (length, 2, 2, n_freq)).reshape(length, ATTN_DH)
    if not rotate:
        ang = jnp.zeros_like(ang)
    cos, sin = jnp.cos(ang), jnp.sin(ang)
    first_half = (jnp.arange(ATTN_DH) % 32) < 16
    sa = jnp.where(first_half, -sin, 0.0)
    sb = jnp.where(first_half, 0.0, sin)
    q_scale = ATTN_DH ** -0.5
    qk = lambda t: jnp.stack([jnp.concatenate([t, t], axis=-1) * q_scale, jnp.concatenate([t, t], axis=-1)])
    return qk(cos), qk(sa), qk(sb)


def _conv_kernel(cur_ref, prev_ref, next_ref, w_ref, b_ref, g_ref, beta_ref, o_ref, gbuf, ybuf, *, tps):
    ti = pl.program_id(0) % tps
    tm = cur_ref.shape[0]
    d = D_MODEL

    def glu(ref):
        return ref[:, :d].astype(F32) * _sigmoid(ref[:, d:].astype(F32))

    gbuf[CONV_HALO:CONV_HALO + tm, :] = glu(cur_ref)
    gbuf[0:CONV_HALO, :] = jnp.where(ti > 0, glu(prev_ref), 0.0)
    gbuf[CONV_HALO + tm:, :] = jnp.where(ti < tps - 1, glu(next_ref), 0.0)
    off = CONV_HALO - CONV_WIDTH // 2
    rows = min(tm, 128)
    for c in range(d // LANES):
        cs = slice(c * LANES, (c + 1) * LANES)
        for r0 in range(0, tm, rows):
            acc = None
            for res in range(8):
                part = None
                for o in range(res, off + CONV_WIDTH, 8):
                    if o < off:
                        continue
                    term = gbuf[r0 + o - res:r0 + o - res + rows + 8, cs] * w_ref[o - off:o - off + 1, cs]
                    part = term if part is None else part + term
                part = part[res:res + rows, :]
                acc = part if acc is None else acc + part
            ybuf[r0:r0 + rows, cs] = acc
    y = _ln(ybuf[...] + b_ref[...]) * g_ref[...] + beta_ref[...]
    o_ref[...] = (y * _sigmoid(y)).astype(o_ref.dtype)


def conv_branch(z, seq_len, conv_w, conv_b, ln_g, ln_b):
    m = z.shape[0]
    tm = min(256, seq_len)
    tps = seq_len // tm
    hb = tm // CONV_HALO
    nblk = m // CONV_HALO
    row = lambda a: a.reshape(1, D_MODEL)
    return pl.pallas_call(
        functools.partial(_conv_kernel, tps=tps),
        out_shape=jax.ShapeDtypeStruct((m, D_MODEL), BF16),
        grid=(m // tm,),
        in_specs=[pl.BlockSpec((tm, 2 * D_MODEL), lambda i: (i, 0)),
                  pl.BlockSpec((CONV_HALO, 2 * D_MODEL), lambda i: (jnp.maximum(i * hb - 1, 0), 0)),
                  pl.BlockSpec((CONV_HALO, 2 * D_MODEL), lambda i: (jnp.minimum((i + 1) * hb, nblk - 1), 0)),
                  pl.BlockSpec((CONV_WIDTH, D_MODEL), lambda i: (0, 0)),
                  pl.BlockSpec((1, D_MODEL), lambda i: (0, 0)),
                  pl.BlockSpec((1, D_MODEL), lambda i: (0, 0)),
                  pl.BlockSpec((1, D_MODEL), lambda i: (0, 0))],
        out_specs=pl.BlockSpec((tm, D_MODEL), lambda i: (i, 0)),
        scratch_shapes=[pltpu.VMEM((tm + 2 * CONV_HALO, D_MODEL), F32), pltpu.VMEM((tm, D_MODEL), F32)],
        compiler_params=_cparams(("parallel",)),
        name="conv_branch",
    )(z, z, z, conv_w, row(conv_b), row(ln_g), row(ln_b))


def ssm_prepare(a_re, a_im, log_dt, b_re, b_im, c_re, c_im):
    a = lax.complex(a_re.astype(F32), a_im.astype(F32))
    dt_a = jnp.exp(log_dt.astype(F32))[..., None] * a
    b = lax.complex(b_re.astype(F32), b_im.astype(F32))
    b_bar = ((jnp.exp(dt_a) - 1.0) / a)[..., None] * b
    cc = lax.complex(c_re.astype(F32), c_im.astype(F32))
    n = SSM_CHUNK
    steps = jnp.arange(n + 1, dtype=F32)
    pw = jnp.exp(steps[None, None, :, None] * dt_a[:, :, None, :])
    exact = functools.partial(jnp.einsum, precision=lax.Precision.HIGHEST)
    lag_resp = jnp.real(exact('dgop,dgkp,dgpi->dgkio', cc, pw[:, :, :n], b_bar))
    idx = np.arange(n)
    lag = idx[None, :] - idx[:, None]
    pick_f = (lag[None] == idx[:, None, None]).astype(np.float32)
    pick_r = (-lag[None] == idx[:, None, None]).astype(np.float32)
    t_sum = (exact('kst,gkio->gsito', pick_f, lag_resp[0]) + exact('kst,gkio->gsito', pick_r, lag_resp[1])
             ).reshape(SSM_GROUPS, GROUP_W, GROUP_W)

    def parts(z, sign):
        return [jnp.real(z), sign * jnp.imag(z)]

    w_f = exact('gsp,gpi->gsip', pw[0][:, n - 1 - idx], b_bar[0]).reshape(SSM_GROUPS, GROUP_W, SSM_STATE)
    w_r = exact('gsp,gpi->gsip', pw[1][:, idx], b_bar[1]).reshape(SSM_GROUPS, GROUP_W, SSM_STATE)
    w_cat = jnp.concatenate(parts(w_f, 1.0) + parts(w_r, 1.0), axis=-1)
    v_f = exact('gop,gtp->gpto', cc[0], pw[0][:, 1 + idx]).reshape(SSM_GROUPS, SSM_STATE, GROUP_W)
    v_r = exact('gop,gtp->gpto', cc[1], pw[1][:, n - idx]).reshape(SSM_GROUPS, SSM_STATE, GROUP_W)
    v_cat = jnp.concatenate(parts(v_f, -1.0) + parts(v_r, -1.0), axis=1)
    a_n = pw[:, :, n]
    mul_same = jnp.concatenate([jnp.real(a_n[0])] * 2 + [jnp.real(a_n[1])] * 2, axis=-1)
    mul_swap = jnp.concatenate([-jnp.imag(a_n[0]), jnp.imag(a_n[0]), -jnp.imag(a_n[1]), jnp.imag(a_n[1])],
                               axis=-1)
    flat = lambda t: t.reshape(1, SSM_GROUPS * GROUP_W)
    return _split_bf16(t_sum) + _split_bf16(w_cat) + _split_bf16(v_cat) + (flat(mul_same), flat(mul_swap))


def _chunk_perm(octet_lanes):
    p = np.zeros((octet_lanes, octet_lanes), np.float32)
    for s in range(SSM_CHUNK):
        for gl in range(LANES // SSM_GROUP):
            for i in range(SSM_GROUP):
                p[s * LANES + gl * SSM_GROUP + i, gl * GROUP_W + s * SSM_GROUP + i] = 1.0
    return p


def _ssm_in_kernel(x_ref, perm_ref, u2_ref, xf):
    rb = x_ref.shape[0]
    xf[...] = x_ref[...].reshape(rb * SSM_CHUNK, LANES).astype(F32)
    lhs = jnp.concatenate([xf[pl.ds(s, rb, stride=SSM_CHUNK), :] for s in range(SSM_CHUNK)], axis=1)
    u2_ref[...] = _dot(lhs.astype(BF16), perm_ref[...]).astype(u2_ref.dtype)


def ssm_chunk_layout(z):
    rows = z.shape[0] // SSM_CHUNK
    z3 = z.reshape(rows, SSM_CHUNK, N_COLS * D_MODEL)
    rb = min(256, rows)
    octets = D_MODEL // LANES
    ow = octets * GROUP_W
    perm = jnp.asarray(_chunk_perm(ow), BF16)
    return pl.pallas_call(
        _ssm_in_kernel,
        out_shape=jax.ShapeDtypeStruct((rows, SSM_GROUPS * GROUP_W), BF16),
        grid=(rows // rb, octets),
        in_specs=[pl.BlockSpec((rb, SSM_CHUNK, LANES), lambda r, j: (r, 0, COL_U * octets + j)),
                  pl.BlockSpec((ow, ow), lambda r, j: (0, 0))],
        out_specs=pl.BlockSpec((rb, ow), lambda r, j: (r, j)),
        scratch_shapes=[pltpu.VMEM((rb * SSM_CHUNK, LANES), F32)],
        compiler_params=_cparams(("parallel", "arbitrary")),
        name="ssm_chunk_layout",
    )(z3, perm)


def _ssm_out_kernel(y_ref, perm_ref, o_ref):
    o_ref[...] = _dot(y_ref[...].astype(BF16), perm_ref[...])


def ssm_token_octets(y2):
    rows = y2.shape[0]
    rb = min(256, rows)
    octets = D_MODEL // LANES
    ow = octets * GROUP_W
    perm_t = jnp.asarray(_chunk_perm(ow).T, BF16)
    blk = pl.BlockSpec((rb, ow), lambda r, j: (r, j))
    return pl.pallas_call(
        _ssm_out_kernel,
        out_shape=jax.ShapeDtypeStruct(y2.shape, F32),
        grid=(rows // rb, octets),
        in_specs=[blk, pl.BlockSpec((ow, ow), lambda r, j: (0, 0))],
        out_specs=blk,
        compiler_params=_cparams(("parallel", "arbitrary")),
        name="ssm_token_octets",
    )(y2, perm_t)


def _ssm_kernel(uc_ref, ul_ref, t_hi_ref, t_lo_ref, w_hi_ref, w_lo_ref, v_hi_ref, v_lo_ref, same_ref, swap_ref,
                yc_ref, yl_ref, sbuf, xbuf, *, batch):
    ncol, srows, _ = sbuf.shape
    nch = srows // SSM_ROWS
    n_ctx = uc_ref.shape[0] // batch
    n_lat = ul_ref.shape[0] // batch
    groups = t_hi_ref.shape[0]
    gsl = lambda k: slice(k * GROUP_W, (k + 1) * GROUP_W)

    def chunk_rows(b, first, count):
        return pl.ds(first * SSM_ROWS + b, count, stride=SSM_ROWS)

    sbuf[...] = jnp.zeros(sbuf.shape, F32)
    for u_ref, first, count in ((uc_ref, 0, n_ctx), (ul_ref, n_ctx, n_lat)):
        for k in range(groups):
            u = u_ref[:, gsl(k)]
            inc = _dot(u, w_hi_ref[k]) + _dot(u, w_lo_ref[k])
            for v in range(2):
                for b in range(batch):
                    sbuf[2 * k + v, chunk_rows(b, first, count), :] = inc[b * count:(b + 1) * count,
                                                                          v * LANES:(v + 1) * LANES]
    for v in range(ncol):
        xbuf[v] = pltpu.roll(sbuf[v], LANES // 2, 1)

    same = [jnp.broadcast_to(same_ref[:, v * LANES:(v + 1) * LANES], (SSM_ROWS, LANES)) for v in range(ncol)]
    swap = [jnp.broadcast_to(swap_ref[:, v * LANES:(v + 1) * LANES], (SSM_ROWS, LANES)) for v in range(ncol)]

    def body(j, state):
        cr = jnp.where(j < n_ctx, n_ctx - 1 - j, nch - 1 - (j - n_ctx))
        new = []
        for v in range(ncol):
            ci = j if v % 2 == 0 else cr
            rs = pl.ds(pl.multiple_of(ci * SSM_ROWS, SSM_ROWS), SSM_ROWS)
            s, sx = state[2 * v], state[2 * v + 1]
            inc = sbuf[v, rs, :]
            sbuf[v, rs, :] = s
            new.append(s * same[v] + sx * swap[v] + inc)
            new.append(sx * same[v] - s * swap[v] + xbuf[v, rs, :])
        return tuple(new)

    lax.fori_loop(0, nch, body, tuple(jnp.zeros((SSM_ROWS, LANES), F32) for _ in range(2 * ncol)), unroll=4)

    for u_ref, y_ref, first, count in ((uc_ref, yc_ref, 0, n_ctx), (ul_ref, yl_ref, n_ctx, n_lat)):
        for k in range(groups):
            u = u_ref[:, gsl(k)]
            st = jnp.concatenate(
                [jnp.concatenate([sbuf[2 * k + v, chunk_rows(b, first, count), :] for b in range(batch)], axis=0)
                 for v in range(2)], axis=1)
            s_hi, s_lo = _split_bf16(st)
            y_ref[:, gsl(k)] = (_dot(u, t_hi_ref[k]) + _dot(u, t_lo_ref[k]) + _dot(s_hi, v_hi_ref[k])
                                + _dot(s_hi, v_lo_ref[k]) + _dot(s_lo, v_hi_ref[k]))


def ssm_mixer(u2c, u2l, ops, batch):
    t_hi, t_lo, w_hi, w_lo, v_hi, v_lo, mul_same, mul_swap = ops
    gw = SSM_GB * GROUP_W
    nch = (u2c.shape[0] + u2l.shape[0]) // batch
    col = lambda a: pl.BlockSpec((a.shape[0], gw), lambda j: (0, j))
    wsp = pl.BlockSpec((SSM_GB, GROUP_W, GROUP_W), lambda j: (j, 0, 0))
    tab = pl.BlockSpec((1, gw), lambda j: (0, j))
    return pl.pallas_call(
        functools.partial(_ssm_kernel, batch=batch),
        out_shape=(jax.ShapeDtypeStruct(u2c.shape, F32), jax.ShapeDtypeStruct(u2l.shape, F32)),
        grid=(SSM_GROUPS // SSM_GB,),
        in_specs=[col(u2c), col(u2l)] + [wsp] * 6 + [tab, tab],
        out_specs=(col(u2c), col(u2l)),
        scratch_shapes=[pltpu.VMEM((gw // LANES, nch * SSM_ROWS, LANES), F32)] * 2,
        compiler_params=_cparams(("parallel",)),
        name="ssm_mixer",
    )(u2c, u2l, t_hi, t_lo, w_hi, w_lo, v_hi, v_lo, mul_same, mul_swap)


def _attn_kernel(*refs, n_src, lam_init, key_block):
    lam_ref, g_ref, q_ref = refs[:3]
    k_refs = refs[3:3 + n_src]
    v_refs = refs[3 + n_src:3 + 2 * n_src]
    o_ref, vext = refs[3 + 2 * n_src], refs[4 + 2 * n_src]
    qi = pl.program_id(2)

    @pl.when(qi == 0)
    def _():
        r0 = 0
        for v_ref in v_refs:
            n = v_ref.shape[0]
            vext[r0:r0 + n, :ATTN_DV] = v_ref[...]
            vext[r0:r0 + n, ATTN_DV:] = jnp.ones((n, ATTN_DV), BF16)
            r0 += n

    q = q_ref[...]
    lane = lax.broadcasted_iota(jnp.int32, q.shape, 1)
    outs = []
    for sub in range(2):
        qs = jnp.where((lane >= ATTN_DH) if sub else (lane < ATTN_DH), q, jnp.zeros_like(q))
        m = acc = None
        r0 = 0
        for k_ref in k_refs:
            n = k_ref.shape[0]
            for c0 in range(0, n, key_block):
                c1 = min(c0 + key_block, n)
                s = _dot_nt(qs, k_ref[c0:c1, :])
                bm = jnp.max(s, axis=-1, keepdims=True)
                m_new = bm if m is None else jnp.maximum(m, bm)
                p = jnp.exp((s - m_new).astype(BF16))
                pv = _dot(p, vext[r0 + c0:r0 + c1, :])
                acc = pv if acc is None else acc * jnp.exp(m - m_new) + pv
                m = m_new
            r0 += n
        outs.append(acc[:, :ATTN_DV] / acc[:, ATTN_DV:])
    o = outs[0] - lam_ref[0] * outs[1]
    o = o * lax.rsqrt(jnp.mean(o * o, axis=-1, keepdims=True) + RMS_EPS) * g_ref[...]
    o_ref[...] = (o * (1.0 - lam_init)).astype(o_ref.dtype)


def diff_attention(qk, key_srcs, lam, subln_g, lam_init, batch):
    lq = qk.shape[0] // batch
    tq = min(1024, lq)
    nq = lq // tq
    lks = [k.shape[0] // batch for k, _ in key_srcs]
    n_src = len(key_srcs)
    hpb = D_MODEL // LANES
    in_specs = [pl.BlockSpec(memory_space=pltpu.SMEM),
                pl.BlockSpec((1, ATTN_DV), lambda b, h, i: (0, 0)),
                pl.BlockSpec((tq, LANES), lambda b, h, i: (b * nq + i, h))]
    in_specs += [pl.BlockSpec((lk, LANES), lambda b, h, i: (b, hpb + h)) for lk in lks]
    in_specs += [pl.BlockSpec((lk, LANES), lambda b, h, i: (b, COL_V * hpb + h)) for lk in lks]
    return pl.pallas_call(
        functools.partial(_attn_kernel, n_src=n_src, lam_init=lam_init, key_block=256),
        out_shape=jax.ShapeDtypeStruct((batch * lq, D_MODEL), BF16),
        grid=(batch, ATTN_HEADS, nq),
        in_specs=in_specs,
        out_specs=pl.BlockSpec((tq, LANES), lambda b, h, i: (b * nq + i, h)),
        scratch_shapes=[pltpu.VMEM((sum(lks), 2 * ATTN_DV), BF16)],
        compiler_params=_cparams(("parallel", "parallel", "arbitrary")),
        name="diff_attention",
    )(lam.reshape(1), subln_g.reshape(1, ATTN_DV), qk, *[k for k, _ in key_srcs], *[z for _, z in key_srcs])


def _merge_kernel(y_ref, u_ref, ac_ref, aa_ref, zg0_ref, zg1_ref, zg2_ref, x_ref,
                  d_ref, wglu_ref, ws_ref, wc_ref, wa_ref, wo_ref, g1_ref, sc2_ref, sh2_ref,
                  lng_ref, lnb_ref, wr_hi_ref, wr_lo_ref, x1_ref, h2_ref, lg_ref, ybuf, *, alpha):
    chunks = y_ref.shape[0]
    for j in range(D_MODEL // LANES):
        for t in range(SSM_CHUNK):
            p0 = (j * SSM_CHUNK + t) * LANES
            ybuf[j, pl.ds(t, chunks, stride=SSM_CHUNK), :] = y_ref[:, p0:p0 + LANES]
    y_ssm_raw = jnp.concatenate([ybuf[j] for j in range(D_MODEL // LANES)], axis=1)
    y = y_ssm_raw + d_ref[...] * u_ref[...].astype(F32)
    g = 0.5 * y * (1.0 + jnp.tanh(math.sqrt(2.0 / math.pi) * (y + 0.044715 * (y * y * y))))
    a_ssm = g * _sigmoid(_dot(g.astype(BF16), wglu_ref[...]))
    y_ssm = _dot(a_ssm.astype(BF16), ws_ref[...])
    y_conv = _dot(ac_ref[...], wc_ref[...])
    y_attn = _dot(aa_ref[...], wa_ref[...])
    m = (_sigmoid(zg0_ref[...].astype(F32)) * y_conv + _sigmoid(zg1_ref[...].astype(F32)) * y_ssm
         + _sigmoid(zg2_ref[...].astype(F32)) * y_attn)
    yo = _dot(m.astype(BF16), wo_ref[...])
    x1 = _ln(alpha * x_ref[...] + g1_ref[...] * yo) * lng_ref[...] + lnb_ref[...]
    x1_ref[...] = x1
    h2 = _ln(x1) * (1.0 + sc2_ref[...]) + sh2_ref[...]
    h2_ref[...] = h2.astype(BF16)
    hi, lo = _split_bf16(h2)
    lg_ref[...] = _dot_nt(wr_hi_ref[...], hi) + _dot_nt(wr_hi_ref[...], lo) + _dot_nt(wr_lo_ref[...], hi)


def merge_residual(y_ssm, z, a_conv, a_attn, x2d, mods, wts, seq_len, alpha):
    m = x2d.shape[0]
    batch = m // seq_len
    tm = min(512, seq_len)
    tps = seq_len // tm
    g1, sc2, sh2 = mods
    nb = g1.shape[0]
    mod_map = (lambda i: (i // tps, 0, 0)) if nb > 1 else (lambda i: (0, 0, 0))
    tile = pl.BlockSpec((tm, D_MODEL), lambda i: (i, 0))
    zcol = lambda cidx: pl.BlockSpec((tm, D_MODEL), lambda i: (i, cidx))
    modspec = pl.BlockSpec((None, 1, D_MODEL), mod_map)
    rowspec = pl.BlockSpec((1, D_MODEL), lambda i: (0, 0))
    wspec = pl.BlockSpec((D_MODEL, D_MODEL), lambda i: (0, 0))
    wrspec = pl.BlockSpec((N_EXPERTS, D_MODEL), lambda i: (0, 0))
    return pl.pallas_call(
        functools.partial(_merge_kernel, alpha=alpha),
        out_shape=(jax.ShapeDtypeStruct((m, D_MODEL), F32), jax.ShapeDtypeStruct((m, D_MODEL), BF16),
                   jax.ShapeDtypeStruct((batch, N_EXPERTS, seq_len), F32)),
        grid=(m // tm,),
        in_specs=[pl.BlockSpec((tm // SSM_CHUNK, SSM_GROUPS * GROUP_W), lambda i: (i, 0)),
                  zcol(COL_U), tile, tile, zcol(COL_GATE0), zcol(COL_GATE0 + 1), zcol(COL_GATE0 + 2),
                  tile, rowspec, wspec, wspec, wspec, wspec, wspec, modspec, modspec, modspec,
                  rowspec, rowspec, wrspec, wrspec],
        out_specs=(tile, tile, pl.BlockSpec((None, N_EXPERTS, tm), lambda i: (i // tps, 0, i % tps))),
        scratch_shapes=[pltpu.VMEM((D_MODEL // LANES, tm, LANES), F32)],
        compiler_params=_cparams(("parallel",)),
        name="merge_residual",
    )(y_ssm, z, a_conv, a_attn, z, z, z, x2d, wts['ssm_d'], wts['w_ssm_glu'], wts['w_ssm_out'],
      wts['w_conv_out'], wts['w_attn_out'], wts['w_o'], g1, sc2, sh2, wts['ln1_g'], wts['ln1_b'],
      wts['wr_hi'], wts['wr_lo'])


def _select_kernel(lg_ref, tri_ref, sel_ref, aff_ref, csum, *, cap):
    lg = lg_ref[...]
    n = lg.shape[1]
    e = jnp.exp(lg - jnp.max(lg, axis=0, keepdims=True))
    aff = e / jnp.sum(e, axis=0, keepdims=True)
    aff_ref[...] = aff
    bits = pltpu.bitcast(aff, jnp.int32)
    capf = float(cap)

    def count(mask):
        return jnp.sum(jnp.where(mask, 1.0, 0.0), axis=1, keepdims=True)

    thr = jnp.zeros((lg.shape[0], 1), jnp.int32)
    for bit in range(30, -1, -1):
        cand = thr | (1 << bit)
        thr = jnp.where(count(bits >= cand) >= capf, cand, thr)
    gt = bits > thr
    eq = bits == thr
    need = capf - count(gt)

    def excl_cumsum(mask):
        x = jnp.where(mask, 1.0, 0.0)
        carry = jnp.zeros((lg.shape[0], 1), F32)
        for j in range(n // LANES):
            blk = x[:, j * LANES:(j + 1) * LANES]
            inc = _dot(blk.astype(BF16), tri_ref[...])
            csum[:, j * LANES:(j + 1) * LANES] = inc - blk + carry
            carry = carry + inc[:, LANES - 1:LANES]
        return csum[...]

    mask = gt | (eq & (excl_cumsum(eq) < need))
    pos = excl_cumsum(mask)
    sel_ref[...] = jnp.where(mask, pos, -1.0)


def route_select(logits_t, cap):
    batch, n_e, n = logits_t.shape
    tri = jnp.asarray(np.triu(np.ones((LANES, LANES), np.float32)), BF16)
    spec = pl.BlockSpec((None, n_e, n), lambda b: (b, 0, 0))
    shp = jax.ShapeDtypeStruct((batch, n_e, n), F32)
    return pl.pallas_call(
        functools.partial(_select_kernel, cap=cap),
        out_shape=(shp, shp),
        grid=(batch,),
        in_specs=[spec, pl.BlockSpec((LANES, LANES), lambda b: (0, 0))],
        out_specs=(spec, spec),
        scratch_shapes=[pltpu.VMEM((n_e, n), F32)],
        compiler_params=_cparams(("parallel",)),
        name="route_select",
    )(logits_t, tri)


def _gather_kernel(span_ref, sel_ref, h_ref, xs_ref, acc_ref, *, slot_block):
    cap = xs_ref.shape[0]
    tile = sel_ref.shape[2]
    n_blocks = cap // slot_block
    base = (pl.program_id(0) * pl.num_programs(1) + pl.program_id(1)) * n_blocks
    for sb in range(n_blocks):
        slot = (lax.broadcasted_iota(jnp.int32, (slot_block, 1), 0) + sb * slot_block).astype(F32)
        acc_ref[...] = jnp.zeros(acc_ref.shape, F32)

        def body(t, carry, slot=slot):
            onehot = jnp.where(sel_ref[t] == slot, 1.0, 0.0).astype(BF16)
            rows = pl.ds(pl.multiple_of(t * tile, tile), tile)
            acc_ref[...] += _dot(onehot, h_ref[rows, :])
            return carry

        lax.fori_loop(span_ref[2 * (base + sb)], span_ref[2 * (base + sb) + 1] + 1, body, 0)
        xs_ref[sb * slot_block:(sb + 1) * slot_block, :] = acc_ref[...].astype(xs_ref.dtype)


def gather_tokens(sel, h2, cap):
    batch, n_e, n = sel.shape
    slot_block = min(LANES, cap)
    tile = min(512, n)
    n_blocks = cap // slot_block
    ends = np.stack([np.arange(n_blocks) * slot_block, np.arange(n_blocks) * slot_block + slot_block - 1], 1)
    token_of = jnp.argmax(sel[..., None] == jnp.asarray(ends.reshape(-1), F32), axis=2)
    spans = (token_of // tile).astype(jnp.int32).reshape(-1)
    grid_spec = pltpu.PrefetchScalarGridSpec(
        num_scalar_prefetch=1,
        grid=(batch, n_e),
        in_specs=[pl.BlockSpec((None, None, n // tile, 1, tile), lambda b, e, sp: (b, e, 0, 0, 0)),
                  pl.BlockSpec((n, D_MODEL), lambda b, e, sp: (b, 0))],
        out_specs=pl.BlockSpec((None, cap, D_MODEL), lambda b, e, sp: (e, b, 0)),
        scratch_shapes=[pltpu.VMEM((slot_block, D_MODEL), F32)],
    )
    return pl.pallas_call(
        functools.partial(_gather_kernel, slot_block=slot_block),
        out_shape=jax.ShapeDtypeStruct((n_e, batch * cap, D_MODEL), BF16),
        grid_spec=grid_spec,
        compiler_params=_cparams(("parallel", "arbitrary")),
        name="gather_tokens",
    )(spans, sel.reshape(batch, n_e, n // tile, 1, tile), h2)


def _experts_kernel(*refs, n_grp, caps, batch):
    wg_ref, wu_ref, wd_ref = refs[:3]
    xs_refs = refs[3:3 + n_grp]
    sel_refs = refs[3 + n_grp:3 + 2 * n_grp]
    aff_refs = refs[3 + 2 * n_grp:3 + 3 * n_grp]
    ye_refs = refs[3 + 3 * n_grp:3 + 4 * n_grp]
    acc_refs = refs[3 + 4 * n_grp:]
    f = pl.program_id(1)
    nf = pl.num_programs(1)
    @pl.when(f == 0)
    def _():
        for acc_ref in acc_refs:
            acc_ref[...] = jnp.zeros(acc_ref.shape, F32)

    wg = wg_ref[...].astype(BF16)
    wu = wu_ref[...].astype(BF16)
    wd = wd_ref[...].astype(BF16)
    for xs_ref, acc_ref in zip(xs_refs, acc_refs):
        rows = xs_ref.shape[0]
        step = min(512, rows)
        for r0 in range(0, rows, step):
            x = xs_ref[r0:r0 + step, :]
            a = _dot(x, wg)
            act = (a * _sigmoid(a) * _dot(x, wu)).astype(BF16)
            acc_ref[r0:r0 + step, :] += _dot(act, wd)

    @pl.when(f == nf - 1)
    def _():
        for sel_ref, aff_ref, ye_ref, acc_ref, cap in zip(sel_refs, aff_refs, ye_refs, acc_refs, caps):
            slot = lax.broadcasted_iota(jnp.int32, (cap, 1), 0).astype(F32)
            for b in range(batch):
                gate = jnp.sum(jnp.where(sel_ref[b] == slot, aff_ref[b], 0.0), axis=1, keepdims=True)
                rs = slice(b * cap, (b + 1) * cap)
                ye_ref[rs, :] = (acc_ref[rs, :] * gate).astype(ye_ref.dtype)


def expert_ffn(groups, w_gate_up, w_down, layer, batch):
    tf = 256
    nf = D_EXPERT // tf
    n_grp = len(groups)
    caps = tuple(g[3] for g in groups)
    xs_specs = [pl.BlockSpec((None, g[0].shape[1], D_MODEL), lambda e, f: (e, 0, 0)) for g in groups]
    sa_specs = [pl.BlockSpec((batch, None, 1, g[1].shape[3]), lambda e, f: (0, e, 0, 0)) for g in groups]
    return pl.pallas_call(
        functools.partial(_experts_kernel, n_grp=n_grp, caps=caps, batch=batch),
        out_shape=tuple(jax.ShapeDtypeStruct(g[0].shape, BF16) for g in groups),
        grid=(N_EXPERTS, nf),
        in_specs=[pl.BlockSpec((None, None, D_MODEL, tf), lambda e, f: (layer, e, 0, f)),
                  pl.BlockSpec((None, None, D_MODEL, tf), lambda e, f: (layer, e, 0, nf + f)),
                  pl.BlockSpec((None, None, tf, D_MODEL), lambda e, f: (layer, e, f, 0))]
        + xs_specs + sa_specs + sa_specs,
        out_specs=tuple(xs_specs),
        scratch_shapes=[pltpu.VMEM(g[0].shape[1:], F32) for g in groups],
        compiler_params=_cparams(("parallel", "arbitrary")),
        name="expert_ffn",
    )(w_gate_up, w_gate_up, w_down, *[g[0] for g in groups], *[g[1] for g in groups], *[g[2] for g in groups])


def _combine_kernel(selt_ref, ye_ref, x1_ref, g2_ref, lng_ref, lnb_ref, o_ref, *, alpha):
    n_e, cap, _ = ye_ref.shape
    st = selt_ref[...]
    slot = lax.broadcasted_iota(jnp.int32, (1, cap), 1).astype(F32)
    acc = None
    for e in range(n_e):
        onehot = jnp.where(st[:, e:e + 1] == slot, 1.0, 0.0).astype(BF16)
        part = _dot(onehot, ye_ref[e])
        acc = part if acc is None else acc + part
    o_ref[...] = _ln(alpha * x1_ref[...] + g2_ref[...] * acc) * lng_ref[...] + lnb_ref[...]


def combine_residual(sel_t, ye, x1, g2, ln_g, ln_b, cap, alpha):
    batch, n, n_e = sel_t.shape
    tn = min(512, n)
    nt = n // tn
    nb = g2.shape[0]
    mod_map = (lambda b, t: (b, 0, 0)) if nb > 1 else (lambda b, t: (0, 0, 0))
    rowspec = pl.BlockSpec((1, D_MODEL), lambda b, t: (0, 0))
    tile = pl.BlockSpec((tn, D_MODEL), lambda b, t: (b * nt + t, 0))
    return pl.pallas_call(
        functools.partial(_combine_kernel, alpha=alpha),
        out_shape=jax.ShapeDtypeStruct((batch * n, D_MODEL), F32),
        grid=(batch, nt),
        in_specs=[pl.BlockSpec((None, tn, n_e), lambda b, t: (b, t, 0)),
                  pl.BlockSpec((n_e, cap, D_MODEL), lambda b, t: (0, b, 0)),
                  tile, pl.BlockSpec((None, 1, D_MODEL), mod_map), rowspec, rowspec],
        out_specs=tile,
        compiler_params=_cparams(("parallel", "arbitrary")),
        name="combine_residual",
    )(sel_t, ye, x1, g2, ln_g, ln_b)


def kernel(x, c, ctx, c_ctx, w_ada, b_ada, w_in, conv_w, conv_b, conv_ln_g, conv_ln_b, w_conv_out, ssm_a_re, ssm_a_im, ssm_log_dt, ssm_b_re, ssm_b_im, ssm_c_re, ssm_c_im, ssm_d, w_ssm_glu, w_ssm_out, attn_lambda, attn_subln_g, w_attn_out, w_o, ln1_g, ln1_b, w_router, w_gate_up, w_down, ln2_g, ln2_b):
    batch, seq_len, d = x.shape
    ctx_len = ctx.shape[1]
    depth = w_in.shape[0]
    alpha = (2.0 * depth) ** 0.25
    cap = CAPACITY_FACTOR * seq_len // N_EXPERTS
    cap_c = CAPACITY_FACTOR * ctx_len // N_EXPERTS
    tabs = rope_tables(seq_len, rotate=True)
    tabs_c = rope_tables(ctx_len, rotate=False)
    row = lambda a: a.reshape(1, d)
    cc = jnp.zeros((8, d), F32).at[:batch].set(c).at[batch].set(c_ctx)

    x2 = x.reshape(batch * seq_len, d)
    xc2 = ctx.reshape(batch * ctx_len, d)
    for l in range(depth):
        ctx_out = l < depth - 1
        lam_init = 0.8 - 0.6 * math.exp(-0.3 * l)
        mod = ada_modulation(cc, w_ada[l], b_ada[l])
        part = lambda rows, k: mod[rows, k * d:(k + 1) * d][:, None, :]
        lat = [part(slice(0, batch), k) for k in range(6)]
        cxm = [part(slice(batch, batch + 1), k) for k in range(6)]
        w_in_b = w_in[l].astype(BF16)
        wr_hi, wr_lo = _split_bf16(w_router[l].T)
        wts = {'ssm_d': row(ssm_d[l]), 'w_ssm_glu': w_ssm_glu[l].astype(BF16), 'w_ssm_out': w_ssm_out[l].astype(BF16),
               'w_conv_out': w_conv_out[l].astype(BF16), 'w_attn_out': w_attn_out[l].astype(BF16),
               'w_o': w_o[l].astype(BF16), 'ln1_g': row(ln1_g[l]), 'ln1_b': row(ln1_b[l]),
               'wr_hi': wr_hi, 'wr_lo': wr_lo}
        lq1, lk1, lq2, lk2 = [attn_lambda[l, i].astype(F32) for i in range(4)]
        lam = jnp.exp(jnp.sum(lq1 * lk1)) - jnp.exp(jnp.sum(lq2 * lk2)) + lam_init
        ssm_ops = ssm_prepare(ssm_a_re[l], ssm_a_im[l], ssm_log_dt[l], ssm_b_re[l], ssm_b_im[l],
                              ssm_c_re[l], ssm_c_im[l])

        z = input_projection(x2, lat[1], lat[0], w_in_b, seq_len)
        qk = input_projection(x2, lat[1], lat[0], w_in_b, seq_len, tabs)
        zc = input_projection(xc2, cxm[1], cxm[0], w_in_b, ctx_len)
        qk_c = input_projection(xc2, cxm[1], cxm[0], w_in_b, ctx_len, tabs_c)

        a_conv = conv_branch(z, seq_len, conv_w[l], conv_b[l], conv_ln_g[l], conv_ln_b[l])
        y_chunks_c, y_chunks = ssm_mixer(ssm_chunk_layout(zc), ssm_chunk_layout(z), ssm_ops, batch)
        y_ssm_c, y_ssm = ssm_token_octets(y_chunks_c), ssm_token_octets(y_chunks)
        a_attn = diff_attention(qk, [(qk_c, zc), (qk, z)], lam, attn_subln_g[l], lam_init, batch)
        x1, h2, lg = merge_residual(y_ssm, z, a_conv, a_attn, x2, (lat[2], lat[4], lat[3]), wts, seq_len, alpha)
        sel, aff = route_select(lg, cap)
        sel4, aff4 = sel[:, :, None, :], aff[:, :, None, :]
        groups = [(gather_tokens(sel, h2, cap), sel4, aff4, cap)]
        if ctx_out:
            a_conv_c = conv_branch(zc, ctx_len, conv_w[l], conv_b[l], conv_ln_g[l], conv_ln_b[l])
            a_attn_c = diff_attention(qk_c, [(qk_c, zc)], lam, attn_subln_g[l], lam_init, batch)
            xc1, hc2, lgc = merge_residual(y_ssm_c, zc, a_conv_c, a_attn_c, xc2,
                                           (cxm[2], cxm[4], cxm[3]), wts, ctx_len, alpha)
            selc, affc = route_select(lgc, cap_c)
            selc4, affc4 = selc[:, :, None, :], affc[:, :, None, :]
            groups.append((gather_tokens(selc, hc2, cap_c), selc4, affc4, cap_c))
        ye = expert_ffn(groups, w_gate_up, w_down, l, batch)
        x2 = combine_residual(jnp.swapaxes(sel, 1, 2), ye[0], x1, lat[5], row(ln2_g[l]), row(ln2_b[l]), cap, alpha)
        if ctx_out:
            xc2 = combine_residual(jnp.swapaxes(selc, 1, 2), ye[1], xc1, cxm[5], row(ln2_g[l]), row(ln2_b[l]),
                                   cap_c, alpha)
    return x2.reshape(batch, seq_len, d)
```

```python
import functools
import math

import jax
import jax.numpy as jnp
import numpy as np
from jax import lax
from jax.experimental import pallas as pl
from jax.experimental.pallas import tpu as pltpu

F32 = jnp.float32
BF16 = jnp.bfloat16

D_MODEL = 1024
GRID_W = 64
N_BRANCH = 3
CONV_WIDTH = 31
CONV_HALO = 16
SSM_GROUP = 16
SSM_GROUPS = D_MODEL // SSM_GROUP
SSM_STATE = 64
SSM_CHUNK = 16
SSM_ROWS = 8
GROUP_W = SSM_CHUNK * SSM_GROUP
SSM_GB = 2
ATTN_HEADS = 8
ATTN_DH = 64
ATTN_DV = 128
ROPE_BASE = 10000.0
N_EXPERTS = 16
D_EXPERT = 2816
CAPACITY_FACTOR = 2
LN_EPS = 1e-6
RMS_EPS = 1e-5
LANES = 128

COL_CONV_V, COL_CONV_G, COL_GATE0, COL_Q, COL_U, COL_K, COL_V = 0, 1, 2, 5, 6, 7, 8
N_COLS = 9

VMEM_LIMIT = 56 * 1024 * 1024


def _cparams(sem, vmem=VMEM_LIMIT):
    return pltpu.CompilerParams(dimension_semantics=sem, vmem_limit_bytes=vmem)


def _ln(x):
    mu = jnp.mean(x, axis=-1, keepdims=True)
    xc = x - mu
    var = jnp.mean(xc * xc, axis=-1, keepdims=True)
    return xc * lax.rsqrt(var + LN_EPS)


def _sigmoid(x):
    return 0.5 * jnp.tanh(0.5 * x) + 0.5


def _dot(a, b):
    return jnp.dot(a, b, preferred_element_type=F32)


def _dot_nt(a, b):
    return lax.dot_general(a, b, (((1,), (1,)), ((), ())), preferred_element_type=F32)


def _split_bf16(x):
    hi = x.astype(BF16)
    lo = (x - hi.astype(F32)).astype(BF16)
    return hi, lo


def _ada_kernel(c_ref, w_ref, b_ref, o_ref):
    c = c_ref[...]
    s = c * _sigmoid(c)
    hi, lo = _split_bf16(s)
    whi, wlo = _split_bf16(w_ref[...])
    o_ref[...] = _dot(hi, whi) + _dot(hi, wlo) + _dot(lo, whi) + b_ref[...]


def ada_modulation(cc, w_ada, b_ada):
    n = w_ada.shape[1]
    tn = 1024
    return pl.pallas_call(
        _ada_kernel,
        out_shape=jax.ShapeDtypeStruct((8, n), F32),
        grid=(n // tn,),
        in_specs=[pl.BlockSpec((8, D_MODEL), lambda j: (0, 0)),
                  pl.BlockSpec((D_MODEL, tn), lambda j: (0, j)),
                  pl.BlockSpec((1, tn), lambda j: (0, j))],
        out_specs=pl.BlockSpec((8, tn), lambda j: (0, j)),
        compiler_params=_cparams(("arbitrary",)),
        name="ada_modulation",
    )(cc, w_ada, b_ada.reshape(1, n))


def _inproj_kernel(x_ref, sc_ref, sh_ref, w_ref, *rest, rotary):
    o_ref, h_ref = rest[-2:]
    j = pl.program_id(1)

    @pl.when(j == 0)
    def _():
        h = _ln(x_ref[...]) * (1.0 + sc_ref[...]) + sh_ref[...]
        h_ref[...] = h.astype(BF16)

    acc = _dot(h_ref[...], w_ref[...])
    if rotary:
        cos, sa, sb = (r[...] for r in rest[:3])
        for hh in range(acc.shape[1] // LANES):
            seg = acc[:, hh * LANES:(hh + 1) * LANES]
            r = seg * cos + pltpu.roll(seg, LANES - 16, 1) * sa + pltpu.roll(seg, 16, 1) * sb
            o_ref[:, hh * LANES:(hh + 1) * LANES] = r.astype(o_ref.dtype)
    else:
        o_ref[...] = acc.astype(o_ref.dtype)


def input_projection(x2d, sc, sh, w_bf16, seq_len, rope_tabs=None):
    m = x2d.shape[0]
    tm = min(1024, seq_len)
    tps = seq_len // tm
    nb = sc.shape[0]
    rotary = rope_tabs is not None
    if rotary:
        n_out = 2
        w_col = lambda j: COL_Q + (COL_K - COL_Q) * j
        o_col = lambda j: j
    else:
        n_out = N_COLS - 2
        w_col = lambda j: j + jnp.where(j >= COL_Q, 1, 0) + jnp.where(j >= COL_K - 1, 1, 0)
        o_col = w_col
    mod_map = (lambda i, j: (i // tps, 0, 0)) if nb > 1 else (lambda i, j: (0, 0, 0))
    tab_spec = pl.BlockSpec((None, tm, LANES), lambda i, j: (j, i % tps, 0))
    return pl.pallas_call(
        functools.partial(_inproj_kernel, rotary=rotary),
        out_shape=jax.ShapeDtypeStruct((m, (2 if rotary else N_COLS) * D_MODEL), BF16),
        grid=(m // tm, n_out),
        in_specs=[pl.BlockSpec((tm, D_MODEL), lambda i, j: (i, 0)),
                  pl.BlockSpec((None, 1, D_MODEL), mod_map),
                  pl.BlockSpec((None, 1, D_MODEL), mod_map),
                  pl.BlockSpec((D_MODEL, D_MODEL), lambda i, j: (0, w_col(j)))] + ([tab_spec] * 3 if rotary else []),
        out_specs=pl.BlockSpec((tm, D_MODEL), lambda i, j: (i, o_col(j))),
        scratch_shapes=[pltpu.VMEM((tm, D_MODEL), BF16)],
        compiler_params=_cparams(("parallel", "arbitrary")),
        name="qk_projection" if rotary else "input_projection",
    )(x2d, sc, sh, w_bf16, *(rope_tabs if rotary else ()))


def rope_tables(length, rotate):
    rows = length // GRID_W
    row = jnp.repeat(jnp.arange(rows), GRID_W)
    col = jnp.tile(jnp.arange(GRID_W), rows)
    n_freq = ATTN_DH // 4
    inv_freq = ROPE_BASE ** (-jnp.arange(n_freq, dtype=F32) / n_freq)
    ang = jnp.stack([row, col], -1).astype(F32)[:, :, None, None] * inv_freq
    ang = jnp.broadcast_to(ang, (length, 2, 2, n_freq)).reshape(length, ATTN_DH)
    if not rotate:
        ang = jnp.zeros_like(ang)
    cos, sin = jnp.cos(ang), jnp.sin(ang)
    first_half = (jnp.arange(ATTN_DH) % 32) < 16
    sa = jnp.where(first_half, -sin, 0.0)
    sb = jnp.where(first_half, 0.0, sin)
    q_scale = ATTN_DH ** -0.5
    qk = lambda t: jnp.stack([jnp.concatenate([t, t], axis=-1) * q_scale, jnp.concatenate([t, t], axis=-1)])
    return qk(cos), qk(sa), qk(sb)


def _conv_kernel(cur_ref, prev_ref, next_ref, w_ref, b_ref, g_ref, beta_ref, o_ref, gbuf, ybuf, *, tps):
    ti = pl.program_id(0) % tps
    tm = cur_ref.shape[0]
    d = D_MODEL

    def glu(ref):
        return ref[:, :d].astype(F32) * _sigmoid(ref[:, d:].astype(F32))

    gbuf[CONV_HALO:CONV_HALO + tm, :] = glu(cur_ref)
    gbuf[0:CONV_HALO, :] = jnp.where(ti > 0, glu(prev_ref), 0.0)
    gbuf[CONV_HALO + tm:, :] = jnp.where(ti < tps - 1, glu(next_ref), 0.0)
    off = CONV_HALO - CONV_WIDTH // 2
    rows = min(tm, 128)
    for c in range(d // LANES):
        cs = slice(c * LANES, (c + 1) * LANES)
        for r0 in range(0, tm, rows):
            acc = None
            for res in range(8):
                part = None
                for o in range(res, off + CONV_WIDTH, 8):
                    if o < off:
                        continue
                    term = gbuf[r0 + o - res:r0 + o - res + rows + 8, cs] * w_ref[o - off:o - off + 1, cs]
                    part = term if part is None else part + term
                part = part[res:res + rows, :]
                acc = part if acc is None else acc + part
            ybuf[r0:r0 + rows, cs] = acc
    y = _ln(ybuf[...] + b_ref[...]) * g_ref[...] + beta_ref[...]
    o_ref[...] = (y * _sigmoid(y)).astype(o_ref.dtype)


def conv_branch(z, seq_len, conv_w, conv_b, ln_g, ln_b):
    m = z.shape[0]
    tm = min(256, seq_len)
    tps = seq_len // tm
    hb = tm // CONV_HALO
    nblk = m // CONV_HALO
    row = lambda a: a.reshape(1, D_MODEL)
    return pl.pallas_call(
        functools.partial(_conv_kernel, tps=tps),
        out_shape=jax.ShapeDtypeStruct((m, D_MODEL), BF16),
        grid=(m // tm,),
        in_specs=[pl.BlockSpec((tm, 2 * D_MODEL), lambda i: (i, 0)),
                  pl.BlockSpec((CONV_HALO, 2 * D_MODEL), lambda i: (jnp.maximum(i * hb - 1, 0), 0)),
                  pl.BlockSpec((CONV_HALO, 2 * D_MODEL), lambda i: (jnp.minimum((i + 1) * hb, nblk - 1), 0)),
                  pl.BlockSpec((CONV_WIDTH, D_MODEL), lambda i: (0, 0)),
                  pl.BlockSpec((1, D_MODEL), lambda i: (0, 0)),
                  pl.BlockSpec((1, D_MODEL), lambda i: (0, 0)),
                  pl.BlockSpec((1, D_MODEL), lambda i: (0, 0))],
        out_specs=pl.BlockSpec((tm, D_MODEL), lambda i: (i, 0)),
        scratch_shapes=[pltpu.VMEM((tm + 2 * CONV_HALO, D_MODEL), F32), pltpu.VMEM((tm, D_MODEL), F32)],
        compiler_params=_cparams(("parallel",)),
        name="conv_branch",
    )(z, z, z, conv_w, row(conv_b), row(ln_g), row(ln_b))


def ssm_prepare(a_re, a_im, log_dt, b_re, b_im, c_re, c_im):
    a = lax.complex(a_re.astype(F32), a_im.astype(F32))
    dt_a = jnp.exp(log_dt.astype(F32))[..., None] * a
    b = lax.complex(b_re.astype(F32), b_im.astype(F32))
    b_bar = ((jnp.exp(dt_a) - 1.0) / a)[..., None] * b
    cc = lax.complex(c_re.astype(F32), c_im.astype(F32))
    n = SSM_CHUNK
    steps = jnp.arange(n + 1, dtype=F32)
    pw = jnp.exp(steps[None, None, :, None] * dt_a[:, :, None, :])
    exact = functools.partial(jnp.einsum, precision=lax.Precision.HIGHEST)
    lag_resp = jnp.real(exact('dgop,dgkp,dgpi->dgkio', cc, pw[:, :, :n], b_bar))
    idx = np.arange(n)
    lag = idx[None, :] - idx[:, None]
    pick_f = (lag[None] == idx[:, None, None]).astype(np.float32)
    pick_r = (-lag[None] == idx[:, None, None]).astype(np.float32)
    t_sum = (exact('kst,gkio->gsito', pick_f, lag_resp[0]) + exact('kst,gkio->gsito', pick_r, lag_resp[1])
             ).reshape(SSM_GROUPS, GROUP_W, GROUP_W)

    def parts(z, sign):
        return [jnp.real(z), sign * jnp.imag(z)]

    w_f = exact('gsp,gpi->gsip', pw[0][:, n - 1 - idx], b_bar[0]).reshape(SSM_GROUPS, GROUP_W, SSM_STATE)
    w_r = exact('gsp,gpi->gsip', pw[1][:, idx], b_bar[1]).reshape(SSM_GROUPS, GROUP_W, SSM_STATE)
    w_cat = jnp.concatenate(parts(w_f, 1.0) + parts(w_r, 1.0), axis=-1)
    v_f = exact('gop,gtp->gpto', cc[0], pw[0][:, 1 + idx]).reshape(SSM_GROUPS, SSM_STATE, GROUP_W)
    v_r = exact('gop,gtp->gpto', cc[1], pw[1][:, n - idx]).reshape(SSM_GROUPS, SSM_STATE, GROUP_W)
    v_cat = jnp.concatenate(parts(v_f, -1.0) + parts(v_r, -1.0), axis=1)
    a_n = pw[:, :, n]
    mul_same = jnp.concatenate([jnp.real(a_n[0])] * 2 + [jnp.real(a_n[1])] * 2, axis=-1)
    mul_swap = jnp.concatenate([-jnp.imag(a_n[0]), jnp.imag(a_n[0]), -jnp.imag(a_n[1]), jnp.imag(a_n[1])],
                               axis=-1)
    flat = lambda t: t.reshape(1, SSM_GROUPS * GROUP_W)
    return _split_bf16(t_sum) + _split_bf16(w_cat) + _split_bf16(v_cat) + (flat(mul_same), flat(mul_swap))


def _chunk_perm(octet_lanes):
    p = np.zeros((octet_lanes, octet_lanes), np.float32)
    for s in range(SSM_CHUNK):
        for gl in range(LANES // SSM_GROUP):
            for i in range(SSM_GROUP):
                p[s * LANES + gl * SSM_GROUP + i, gl * GROUP_W + s * SSM_GROUP + i] = 1.0
    return p


def _ssm_in_kernel(x_ref, perm_ref, u2_ref, xf):
    rb = x_ref.shape[0]
    xf[...] = x_ref[...].reshape(rb * SSM_CHUNK, LANES).astype(F32)
    lhs = jnp.concatenate([xf[pl.ds(s, rb, stride=SSM_CHUNK), :] for s in range(SSM_CHUNK)], axis=1)
    u2_ref[...] = _dot(lhs.astype(BF16), perm_ref[...]).astype(u2_ref.dtype)


def ssm_chunk_layout(z):
    rows = z.shape[0] // SSM_CHUNK
    z3 = z.reshape(rows, SSM_CHUNK, N_COLS * D_MODEL)
    rb = min(256, rows)
    octets = D_MODEL // LANES
    ow = octets * GROUP_W
    perm = jnp.asarray(_chunk_perm(ow), BF16)
    return pl.pallas_call(
        _ssm_in_kernel,
        out_shape=jax.ShapeDtypeStruct((rows, SSM_GROUPS * GROUP_W), BF16),
        grid=(rows // rb, octets),
        in_specs=[pl.BlockSpec((rb, SSM_CHUNK, LANES), lambda r, j: (r, 0, COL_U * octets + j)),
                  pl.BlockSpec((ow, ow), lambda r, j: (0, 0))],
        out_specs=pl.BlockSpec((rb, ow), lambda r, j: (r, j)),
        scratch_shapes=[pltpu.VMEM((rb * SSM_CHUNK, LANES), F32)],
        compiler_params=_cparams(("parallel", "arbitrary")),
        name="ssm_chunk_layout",
    )(z3, perm)


def _ssm_out_kernel(y_ref, perm_ref, o_ref):
    o_ref[...] = _dot(y_ref[...].astype(BF16), perm_ref[...])


def ssm_token_octets(y2):
    rows = y2.shape[0]
    rb = min(256, rows)
    octets = D_MODEL // LANES
    ow = octets * GROUP_W
    perm_t = jnp.asarray(_chunk_perm(ow).T, BF16)
    blk = pl.BlockSpec((rb, ow), lambda r, j: (r, j))
    return pl.pallas_call(
        _ssm_out_kernel,
        out_shape=jax.ShapeDtypeStruct(y2.shape, F32),
        grid=(rows // rb, octets),
        in_specs=[blk, pl.BlockSpec((ow, ow), lambda r, j: (0, 0))],
        out_specs=blk,
        compiler_params=_cparams(("parallel", "arbitrary")),
        name="ssm_token_octets",
    )(y2, perm_t)


def _ssm_kernel(uc_ref, ul_ref, t_hi_ref, t_lo_ref, w_hi_ref, w_lo_ref, v_hi_ref, v_lo_ref, same_ref, swap_ref,
                yc_ref, yl_ref, sbuf, xbuf, *, batch):
    ncol, srows, _ = sbuf.shape
    nch = srows // SSM_ROWS
    n_ctx = uc_ref.shape[0] // batch
    n_lat = ul_ref.shape[0] // batch
    groups = t_hi_ref.shape[0]
    gsl = lambda k: slice(k * GROUP_W, (k + 1) * GROUP_W)

    def chunk_rows(b, first, count):
        return pl.ds(first * SSM_ROWS + b, count, stride=SSM_ROWS)

    sbuf[...] = jnp.zeros(sbuf.shape, F32)
    for u_ref, first, count in ((uc_ref, 0, n_ctx), (ul_ref, n_ctx, n_lat)):
        for k in range(groups):
            u = u_ref[:, gsl(k)]
            inc = _dot(u, w_hi_ref[k])
            for v in range(2):
                for b in range(batch):
                    sbuf[2 * k + v, chunk_rows(b, first, count), :] = inc[b * count:(b + 1) * count,
                                                                          v * LANES:(v + 1) * LANES]
    for v in range(ncol):
        xbuf[v] = pltpu.roll(sbuf[v], LANES // 2, 1)

    same = [jnp.broadcast_to(same_ref[:, v * LANES:(v + 1) * LANES], (SSM_ROWS, LANES)) for v in range(ncol)]
    swap = [jnp.broadcast_to(swap_ref[:, v * LANES:(v + 1) * LANES], (SSM_ROWS, LANES)) for v in range(ncol)]

    def body(j, state):
        cr = jnp.where(j < n_ctx, n_ctx - 1 - j, nch - 1 - (j - n_ctx))
        new = []
        for v in range(ncol):
            ci = j if v % 2 == 0 else cr
            rs = pl.ds(pl.multiple_of(ci * SSM_ROWS, SSM_ROWS), SSM_ROWS)
            s, sx = state[2 * v], state[2 * v + 1]
            inc = sbuf[v, rs, :]
            sbuf[v, rs, :] = s
            new.append(s * same[v] + sx * swap[v] + inc)
            new.append(sx * same[v] - s * swap[v] + xbuf[v, rs, :])
        return tuple(new)

    lax.fori_loop(0, nch, body, tuple(jnp.zeros((SSM_ROWS, LANES), F32) for _ in range(2 * ncol)), unroll=4)

    for u_ref, y_ref, first, count in ((uc_ref, yc_ref, 0, n_ctx), (ul_ref, yl_ref, n_ctx, n_lat)):
        for k in range(groups):
            u = u_ref[:, gsl(k)]
            st = jnp.concatenate(
                [jnp.concatenate([sbuf[2 * k + v, chunk_rows(b, first, count), :] for b in range(batch)], axis=0)
                 for v in range(2)], axis=1)
            s_hi, s_lo = _split_bf16(st)
            y_ref[:, gsl(k)] = _dot(u, t_hi_ref[k]) + _dot(s_hi, v_hi_ref[k]) + _dot(s_lo, v_hi_ref[k])


def ssm_mixer(u2c, u2l, ops, batch):
    t_hi, t_lo, w_hi, w_lo, v_hi, v_lo, mul_same, mul_swap = ops
    gw = SSM_GB * GROUP_W
    nch = (u2c.shape[0] + u2l.shape[0]) // batch
    col = lambda a: pl.BlockSpec((a.shape[0], gw), lambda j: (0, j))
    wsp = pl.BlockSpec((SSM_GB, GROUP_W, GROUP_W), lambda j: (j, 0, 0))
    tab = pl.BlockSpec((1, gw), lambda j: (0, j))
    return pl.pallas_call(
        functools.partial(_ssm_kernel, batch=batch),
        out_shape=(jax.ShapeDtypeStruct(u2c.shape, F32), jax.ShapeDtypeStruct(u2l.shape, F32)),
        grid=(SSM_GROUPS // SSM_GB,),
        in_specs=[col(u2c), col(u2l)] + [wsp] * 6 + [tab, tab],
        out_specs=(col(u2c), col(u2l)),
        scratch_shapes=[pltpu.VMEM((gw // LANES, nch * SSM_ROWS, LANES), F32)] * 2,
        compiler_params=_cparams(("parallel",)),
        name="ssm_mixer",
    )(u2c, u2l, t_hi, t_lo, w_hi, w_lo, v_hi, v_lo, mul_same, mul_swap)


def _attn_kernel(*refs, n_src, lam_init, key_block):
    lam_ref, g_ref, q_ref = refs[:3]
    k_refs = refs[3:3 + n_src]
    v_refs = refs[3 + n_src:3 + 2 * n_src]
    o_ref, vext = refs[3 + 2 * n_src], refs[4 + 2 * n_src]
    qi = pl.program_id(2)

    @pl.when(qi == 0)
    def _():
        r0 = 0
        for v_ref in v_refs:
            n = v_ref.shape[0]
            vext[r0:r0 + n, :ATTN_DV] = v_ref[...]
            vext[r0:r0 + n, ATTN_DV:] = jnp.ones((n, ATTN_DV), BF16)
            r0 += n

    q = q_ref[...]
    lane = lax.broadcasted_iota(jnp.int32, q.shape, 1)
    outs = []
    for sub in range(2):
        qs = jnp.where((lane >= ATTN_DH) if sub else (lane < ATTN_DH), q, jnp.zeros_like(q))
        m = acc = None
        r0 = 0
        for k_ref in k_refs:
            n = k_ref.shape[0]
            for c0 in range(0, n, key_block):
                c1 = min(c0 + key_block, n)
                s = _dot_nt(qs, k_ref[c0:c1, :])
                bm = jnp.max(s, axis=-1, keepdims=True)
                m_new = bm if m is None else jnp.maximum(m, bm)
                p = jnp.exp((s - m_new).astype(BF16))
                pv = _dot(p, vext[r0 + c0:r0 + c1, :])
                acc = pv if acc is None else acc * jnp.exp(m - m_new) + pv
                m = m_new
            r0 += n
        outs.append(acc[:, :ATTN_DV] / acc[:, ATTN_DV:])
    o = outs[0] - lam_ref[0] * outs[1]
    o = o * lax.rsqrt(jnp.mean(o * o, axis=-1, keepdims=True) + RMS_EPS) * g_ref[...]
    o_ref[...] = (o * (1.0 - lam_init)).astype(o_ref.dtype)


def diff_attention(qk, key_srcs, lam, subln_g, lam_init, batch):
    lq = qk.shape[0] // batch
    tq = min(1024, lq)
    nq = lq // tq
    lks = [k.shape[0] // batch for k, _ in key_srcs]
    n_src = len(key_srcs)
    hpb = D_MODEL // LANES
    in_specs = [pl.BlockSpec(memory_space=pltpu.SMEM),
                pl.BlockSpec((1, ATTN_DV), lambda b, h, i: (0, 0)),
                pl.BlockSpec((tq, LANES), lambda b, h, i: (b * nq + i, h))]
    in_specs += [pl.BlockSpec((lk, LANES), lambda b, h, i: (b, hpb + h)) for lk in lks]
    in_specs += [pl.BlockSpec((lk, LANES), lambda b, h, i: (b, COL_V * hpb + h)) for lk in lks]
    return pl.pallas_call(
        functools.partial(_attn_kernel, n_src=n_src, lam_init=lam_init, key_block=256),
        out_shape=jax.ShapeDtypeStruct((batch * lq, D_MODEL), BF16),
        grid=(batch, ATTN_HEADS, nq),
        in_specs=in_specs,
        out_specs=pl.BlockSpec((tq, LANES), lambda b, h, i: (b * nq + i, h)),
        scratch_shapes=[pltpu.VMEM((sum(lks), 2 * ATTN_DV), BF16)],
        compiler_params=_cparams(("parallel", "parallel", "arbitrary")),
        name="diff_attention",
    )(lam.reshape(1), subln_g.reshape(1, ATTN_DV), qk, *[k for k, _ in key_srcs], *[z for _, z in key_srcs])


def _merge_kernel(y_ref, u_ref, ac_ref, aa_ref, zg0_ref, zg1_ref, zg2_ref, x_ref,
                  d_ref, wglu_ref, ws_ref, wc_ref, wa_ref, wo_ref, g1_ref, sc2_ref, sh2_ref,
                  lng_ref, lnb_ref, wr_hi_ref, wr_lo_ref, x1_ref, h2_ref, lg_ref, ybuf, *, alpha):
    chunks = y_ref.shape[0]
    for j in range(D_MODEL // LANES):
        for t in range(SSM_CHUNK):
            p0 = (j * SSM_CHUNK + t) * LANES
            ybuf[j, pl.ds(t, chunks, stride=SSM_CHUNK), :] = y_ref[:, p0:p0 + LANES]
    y_ssm_raw = jnp.concatenate([ybuf[j] for j in range(D_MODEL // LANES)], axis=1)
    y = y_ssm_raw + d_ref[...] * u_ref[...].astype(F32)
    g = 0.5 * y * (1.0 + jnp.tanh(math.sqrt(2.0 / math.pi) * (y + 0.044715 * (y * y * y))))
    a_ssm = g * _sigmoid(_dot(g.astype(BF16), wglu_ref[...]))
    y_ssm = _dot(a_ssm.astype(BF16), ws_ref[...])
    y_conv = _dot(ac_ref[...], wc_ref[...])
    y_attn = _dot(aa_ref[...], wa_ref[...])
    m = (_sigmoid(zg0_ref[...].astype(F32)) * y_conv + _sigmoid(zg1_ref[...].astype(F32)) * y_ssm
         + _sigmoid(zg2_ref[...].astype(F32)) * y_attn)
    yo = _dot(m.astype(BF16), wo_ref[...])
    x1 = _ln(alpha * x_ref[...] + g1_ref[...] * yo) * lng_ref[...] + lnb_ref[...]
    x1_ref[...] = x1
    h2 = _ln(x1) * (1.0 + sc2_ref[...]) + sh2_ref[...]
    h2_ref[...] = h2.astype(BF16)
    hi, lo = _split_bf16(h2)
    lg_ref[...] = _dot_nt(wr_hi_ref[...], hi) + _dot_nt(wr_hi_ref[...], lo) + _dot_nt(wr_lo_ref[...], hi)


def merge_residual(y_ssm, z, a_conv, a_attn, x2d, mods, wts, seq_len, alpha):
    m = x2d.shape[0]
    batch = m // seq_len
    tm = min(512, seq_len)
    tps = seq_len // tm
    g1, sc2, sh2 = mods
    nb = g1.shape[0]
    mod_map = (lambda i: (i // tps, 0, 0)) if nb > 1 else (lambda i: (0, 0, 0))
    tile = pl.BlockSpec((tm, D_MODEL), lambda i: (i, 0))
    zcol = lambda cidx: pl.BlockSpec((tm, D_MODEL), lambda i: (i, cidx))
    modspec = pl.BlockSpec((None, 1, D_MODEL), mod_map)
    rowspec = pl.BlockSpec((1, D_MODEL), lambda i: (0, 0))
    wspec = pl.BlockSpec((D_MODEL, D_MODEL), lambda i: (0, 0))
    wrspec = pl.BlockSpec((N_EXPERTS, D_MODEL), lambda i: (0, 0))
    return pl.pallas_call(
        functools.partial(_merge_kernel, alpha=alpha),
        out_shape=(jax.ShapeDtypeStruct((m, D_MODEL), F32), jax.ShapeDtypeStruct((m, D_MODEL), BF16),
                   jax.ShapeDtypeStruct((batch, N_EXPERTS, seq_len), F32)),
        grid=(m // tm,),
        in_specs=[pl.BlockSpec((tm // SSM_CHUNK, SSM_GROUPS * GROUP_W), lambda i: (i, 0)),
                  zcol(COL_U), tile, tile, zcol(COL_GATE0), zcol(COL_GATE0 + 1), zcol(COL_GATE0 + 2),
                  tile, rowspec, wspec, wspec, wspec, wspec, wspec, modspec, modspec, modspec,
                  rowspec, rowspec, wrspec, wrspec],
        out_specs=(tile, tile, pl.BlockSpec((None, N_EXPERTS, tm), lambda i: (i // tps, 0, i % tps))),
        scratch_shapes=[pltpu.VMEM((D_MODEL // LANES, tm, LANES), F32)],
        compiler_params=_cparams(("parallel",)),
        name="merge_residual",
    )(y_ssm, z, a_conv, a_attn, z, z, z, x2d, wts['ssm_d'], wts['w_ssm_glu'], wts['w_ssm_out'],
      wts['w_conv_out'], wts['w_attn_out'], wts['w_o'], g1, sc2, sh2, wts['ln1_g'], wts['ln1_b'],
      wts['wr_hi'], wts['wr_lo'])


def _select_kernel(lg_ref, tri_ref, sel_ref, aff_ref, csum, *, cap):
    lg = lg_ref[...]
    n = lg.shape[1]
    e = jnp.exp(lg - jnp.max(lg, axis=0, keepdims=True))
    aff = e / jnp.sum(e, axis=0, keepdims=True)
    aff_ref[...] = aff
    bits = pltpu.bitcast(aff, jnp.int32)
    capf = float(cap)

    def count(mask):
        return jnp.sum(jnp.where(mask, 1.0, 0.0), axis=1, keepdims=True)

    thr = jnp.zeros((lg.shape[0], 1), jnp.int32)
    for bit in range(30, -1, -1):
        cand = thr | (1 << bit)
        thr = jnp.where(count(bits >= cand) >= capf, cand, thr)
    gt = bits > thr
    eq = bits == thr
    need = capf - count(gt)

    def excl_cumsum(mask):
        x = jnp.where(mask, 1.0, 0.0)
        carry = jnp.zeros((lg.shape[0], 1), F32)
        for j in range(n // LANES):
            blk = x[:, j * LANES:(j + 1) * LANES]
            inc = _dot(blk.astype(BF16), tri_ref[...])
            csum[:, j * LANES:(j + 1) * LANES] = inc - blk + carry
            carry = carry + inc[:, LANES - 1:LANES]
        return csum[...]

    mask = gt | (eq & (excl_cumsum(eq) < need))
    pos = excl_cumsum(mask)
    sel_ref[...] = jnp.where(mask, pos, -1.0)


def route_select(logits_t, cap):
    batch, n_e, n = logits_t.shape
    tri = jnp.asarray(np.triu(np.ones((LANES, LANES), np.float32)), BF16)
    spec = pl.BlockSpec((None, n_e, n), lambda b: (b, 0, 0))
    shp = jax.ShapeDtypeStruct((batch, n_e, n), F32)
    return pl.pallas_call(
        functools.partial(_select_kernel, cap=cap),
        out_shape=(shp, shp),
        grid=(batch,),
        in_specs=[spec, pl.BlockSpec((LANES, LANES), lambda b: (0, 0))],
        out_specs=(spec, spec),
        scratch_shapes=[pltpu.VMEM((n_e, n), F32)],
        compiler_params=_cparams(("parallel",)),
        name="route_select",
    )(logits_t, tri)


def _gather_kernel(span_ref, sel_ref, h_ref, xs_ref, acc_ref, *, slot_block):
    cap = xs_ref.shape[0]
    tile = sel_ref.shape[2]
    n_blocks = cap // slot_block
    base = (pl.program_id(0) * pl.num_programs(1) + pl.program_id(1)) * n_blocks
    for sb in range(n_blocks):
        slot = (lax.broadcasted_iota(jnp.int32, (slot_block, 1), 0) + sb * slot_block).astype(F32)
        acc_ref[...] = jnp.zeros(acc_ref.shape, F32)

        def body(t, carry, slot=slot):
            onehot = jnp.where(sel_ref[t] == slot, 1.0, 0.0).astype(BF16)
            rows = pl.ds(pl.multiple_of(t * tile, tile), tile)
            acc_ref[...] += _dot(onehot, h_ref[rows, :])
            return carry

        lax.fori_loop(span_ref[2 * (base + sb)], span_ref[2 * (base + sb) + 1] + 1, body, 0)
        xs_ref[sb * slot_block:(sb + 1) * slot_block, :] = acc_ref[...].astype(xs_ref.dtype)


def gather_tokens(sel, h2, cap):
    batch, n_e, n = sel.shape
    slot_block = min(LANES, cap)
    tile = min(512, n)
    n_blocks = cap // slot_block
    ends = np.stack([np.arange(n_blocks) * slot_block, np.arange(n_blocks) * slot_block + slot_block - 1], 1)
    token_of = jnp.argmax(sel[..., None] == jnp.asarray(ends.reshape(-1), F32), axis=2)
    spans = (token_of // tile).astype(jnp.int32).reshape(-1)
    grid_spec = pltpu.PrefetchScalarGridSpec(
        num_scalar_prefetch=1,
        grid=(batch, n_e),
        in_specs=[pl.BlockSpec((None, None, n // tile, 1, tile), lambda b, e, sp: (b, e, 0, 0, 0)),
                  pl.BlockSpec((n, D_MODEL), lambda b, e, sp: (b, 0))],
        out_specs=pl.BlockSpec((None, cap, D_MODEL), lambda b, e, sp: (e, b, 0)),
        scratch_shapes=[pltpu.VMEM((slot_block, D_MODEL), F32)],
    )
    return pl.pallas_call(
        functools.partial(_gather_kernel, slot_block=slot_block),
        out_shape=jax.ShapeDtypeStruct((n_e, batch * cap, D_MODEL), BF16),
        grid_spec=grid_spec,
        compiler_params=_cparams(("parallel", "arbitrary")),
        name="gather_tokens",
    )(spans, sel.reshape(batch, n_e, n // tile, 1, tile), h2)


def _experts_kernel(*refs, n_grp, caps, batch):
    wg_ref, wu_ref, wd_ref = refs[:3]
    xs_refs = refs[3:3 + n_grp]
    sel_refs = refs[3 + n_grp:3 + 2 * n_grp]
    aff_refs = refs[3 + 2 * n_grp:3 + 3 * n_grp]
    ye_refs = refs[3 + 3 * n_grp:3 + 4 * n_grp]
    acc_refs = refs[3 + 4 * n_grp:]
    f = pl.program_id(1)
    nf = pl.num_programs(1)
    @pl.when(f == 0)
    def _():
        for acc_ref in acc_refs:
            acc_ref[...] = jnp.zeros(acc_ref.shape, F32)

    wg = wg_ref[...].astype(BF16)
    wu = wu_ref[...].astype(BF16)
    wd = wd_ref[...].astype(BF16)
    for xs_ref, acc_ref in zip(xs_refs, acc_refs):
        rows = xs_ref.shape[0]
        step = min(512, rows)
        for r0 in range(0, rows, step):
            x = xs_ref[r0:r0 + step, :]
            a = _dot(x, wg)
            act = (a * _sigmoid(a) * _dot(x, wu)).astype(BF16)
            acc_ref[r0:r0 + step, :] += _dot(act, wd)

    @pl.when(f == nf - 1)
    def _():
        for sel_ref, aff_ref, ye_ref, acc_ref, cap in zip(sel_refs, aff_refs, ye_refs, acc_refs, caps):
            slot = lax.broadcasted_iota(jnp.int32, (cap, 1), 0).astype(F32)
            for b in range(batch):
                gate = jnp.sum(jnp.where(sel_ref[b] == slot, aff_ref[b], 0.0), axis=1, keepdims=True)
                rs = slice(b * cap, (b + 1) * cap)
                ye_ref[rs, :] = (acc_ref[rs, :] * gate).astype(ye_ref.dtype)


def expert_ffn(groups, w_gate_up, w_down, layer, batch):
    tf = 256
    nf = D_EXPERT // tf
    n_grp = len(groups)
    caps = tuple(g[3] for g in groups)
    xs_specs = [pl.BlockSpec((None, g[0].shape[1], D_MODEL), lambda e, f: (e, 0, 0)) for g in groups]
    sa_specs = [pl.BlockSpec((batch, None, 1, g[1].shape[3]), lambda e, f: (0, e, 0, 0)) for g in groups]
    return pl.pallas_call(
        functools.partial(_experts_kernel, n_grp=n_grp, caps=caps, batch=batch),
        out_shape=tuple(jax.ShapeDtypeStruct(g[0].shape, BF16) for g in groups),
        grid=(N_EXPERTS, nf),
        in_specs=[pl.BlockSpec((None, None, D_MODEL, tf), lambda e, f: (layer, e, 0, f)),
                  pl.BlockSpec((None, None, D_MODEL, tf), lambda e, f: (layer, e, 0, nf + f)),
                  pl.BlockSpec((None, None, tf, D_MODEL), lambda e, f: (layer, e, f, 0))]
        + xs_specs + sa_specs + sa_specs,
        out_specs=tuple(xs_specs),
        scratch_shapes=[pltpu.VMEM(g[0].shape[1:], F32) for g in groups],
        compiler_params=_cparams(("parallel", "arbitrary")),
        name="expert_ffn",
    )(w_gate_up, w_gate_up, w_down, *[g[0] for g in groups], *[g[1] for g in groups], *[g[2] for g in groups])


def _combine_kernel(selt_ref, ye_ref, x1_ref, g2_ref, lng_ref, lnb_ref, o_ref, *, alpha):
    n_e, cap, _ = ye_ref.shape
    st = selt_ref[...]
    slot = lax.broadcasted_iota(jnp.int32, (1, cap), 1).astype(F32)
    acc = None
    for e in range(n_e):
        onehot = jnp.where(st[:, e:e + 1] == slot, 1.0, 0.0).astype(BF16)
        part = _dot(onehot, ye_ref[e])
        acc = part if acc is None else acc + part
    o_ref[...] = _ln(alpha * x1_ref[...] + g2_ref[...] * acc) * lng_ref[...] + lnb_ref[...]


def combine_residual(sel_t, ye, x1, g2, ln_g, ln_b, cap, alpha):
    batch, n, n_e = sel_t.shape
    tn = min(512, n)
    nt = n // tn
    nb = g2.shape[0]
    mod_map = (lambda b, t: (b, 0, 0)) if nb > 1 else (lambda b, t: (0, 0, 0))
    rowspec = pl.BlockSpec((1, D_MODEL), lambda b, t: (0, 0))
    tile = pl.BlockSpec((tn, D_MODEL), lambda b, t: (b * nt + t, 0))
    return pl.pallas_call(
        functools.partial(_combine_kernel, alpha=alpha),
        out_shape=jax.ShapeDtypeStruct((batch * n, D_MODEL), F32),
        grid=(batch, nt),
        in_specs=[pl.BlockSpec((None, tn, n_e), lambda b, t: (b, t, 0)),
                  pl.BlockSpec((n_e, cap, D_MODEL), lambda b, t: (0, b, 0)),
                  tile, pl.BlockSpec((None, 1, D_MODEL), mod_map), rowspec, rowspec],
        out_specs=tile,
        compiler_params=_cparams(("parallel", "arbitrary")),
        name="combine_residual",
    )(sel_t, ye, x1, g2, ln_g, ln_b)


def kernel(x, c, ctx, c_ctx, w_ada, b_ada, w_in, conv_w, conv_b, conv_ln_g, conv_ln_b, w_conv_out, ssm_a_re, ssm_a_im, ssm_log_dt, ssm_b_re, ssm_b_im, ssm_c_re, ssm_c_im, ssm_d, w_ssm_glu, w_ssm_out, attn_lambda, attn_subln_g, w_attn_out, w_o, ln1_g, ln1_b, w_router, w_gate_up, w_down, ln2_g, ln2_b):
    batch, seq_len, d = x.shape
    ctx_len = ctx.shape[1]
    depth = w_in.shape[0]
    alpha = (2.0 * depth) ** 0.25
    cap = CAPACITY_FACTOR * seq_len // N_EXPERTS
    cap_c = CAPACITY_FACTOR * ctx_len // N_EXPERTS
    tabs = rope_tables(seq_len, rotate=True)
    tabs_c = rope_tables(ctx_len, rotate=False)
    row = lambda a: a.reshape(1, d)
    cc = jnp.zeros((8, d), F32).at[:batch].set(c).at[batch].set(c_ctx)

    x2 = x.reshape(batch * seq_len, d)
    xc2 = ctx.reshape(batch * ctx_len, d)
    for l in range(depth):
        ctx_out = l < depth - 1
        lam_init = 0.8 - 0.6 * math.exp(-0.3 * l)
        mod = ada_modulation(cc, w_ada[l], b_ada[l])
        part = lambda rows, k: mod[rows, k * d:(k + 1) * d][:, None, :]
        lat = [part(slice(0, batch), k) for k in range(6)]
        cxm = [part(slice(batch, batch + 1), k) for k in range(6)]
        w_in_b = w_in[l].astype(BF16)
        wr_hi, wr_lo = _split_bf16(w_router[l].T)
        wts = {'ssm_d': row(ssm_d[l]), 'w_ssm_glu': w_ssm_glu[l].astype(BF16), 'w_ssm_out': w_ssm_out[l].astype(BF16),
               'w_conv_out': w_conv_out[l].astype(BF16), 'w_attn_out': w_attn_out[l].astype(BF16),
               'w_o': w_o[l].astype(BF16), 'ln1_g': row(ln1_g[l]), 'ln1_b': row(ln1_b[l]),
               'wr_hi': wr_hi, 'wr_lo': wr_lo}
        lq1, lk1, lq2, lk2 = [attn_lambda[l, i].astype(F32) for i in range(4)]
        lam = jnp.exp(jnp.sum(lq1 * lk1)) - jnp.exp(jnp.sum(lq2 * lk2)) + lam_init
        ssm_ops = ssm_prepare(ssm_a_re[l], ssm_a_im[l], ssm_log_dt[l], ssm_b_re[l], ssm_b_im[l],
                              ssm_c_re[l], ssm_c_im[l])

        z = input_projection(x2, lat[1], lat[0], w_in_b, seq_len)
        qk = input_projection(x2, lat[1], lat[0], w_in_b, seq_len, tabs)
        zc = input_projection(xc2, cxm[1], cxm[0], w_in_b, ctx_len)
        qk_c = input_projection(xc2, cxm[1], cxm[0], w_in_b, ctx_len, tabs_c)

        a_conv = conv_branch(z, seq_len, conv_w[l], conv_b[l], conv_ln_g[l], conv_ln_b[l])
        y_chunks_c, y_chunks = ssm_mixer(ssm_chunk_layout(zc), ssm_chunk_layout(z), ssm_ops, batch)
        y_ssm_c, y_ssm = ssm_token_octets(y_chunks_c), ssm_token_octets(y_chunks)
        a_attn = diff_attention(qk, [(qk_c, zc), (qk, z)], lam, attn_subln_g[l], lam_init, batch)
        x1, h2, lg = merge_residual(y_ssm, z, a_conv, a_attn, x2, (lat[2], lat[4], lat[3]), wts, seq_len, alpha)
        sel, aff = route_select(lg, cap)
        sel4, aff4 = sel[:, :, None, :], aff[:, :, None, :]
        groups = [(gather_tokens(sel, h2, cap), sel4, aff4, cap)]
        if ctx_out:
            a_conv_c = conv_branch(zc, ctx_len, conv_w[l], conv_b[l], conv_ln_g[l], conv_ln_b[l])
            a_attn_c = diff_attention(qk_c, [(qk_c, zc)], lam, attn_subln_g[l], lam_init, batch)
            xc1, hc2, lgc = merge_residual(y_ssm_c, zc, a_conv_c, a_attn_c, xc2,
                                           (cxm[2], cxm[4], cxm[3]), wts, ctx_len, alpha)
            selc, affc = route_select(lgc, cap_c)
            selc4, affc4 = selc[:, :, None, :], affc[:, :, None, :]
            groups.append((gather_tokens(selc, hc2, cap_c), selc4, affc4, cap_c))
        ye = expert_ffn(groups, w_gate_up, w_down, l, batch)
        x2 = combine_residual(jnp.swapaxes(sel, 1, 2), ye[0], x1, lat[5], row(ln2_g[l]), row(ln2_b[l]), cap, alpha)
        if ctx_out:
            xc2 = combine_residual(jnp.swapaxes(selc, 1, 2), ye[1], xc1, cxm[5], row(ln2_g[l]), row(ln2_b[l]),
                                   cap_c, alpha)
    return x2.reshape(batch, seq_len, d)
```
